```python
import math
import jax, jax.numpy as jnp
from jax import lax
import numpy as np

D_MODEL = 2048
BATCH = 4
SEQ = 4096
DEPTH = 1

CHUNK = 64
Q_BLOCK = 128
HEAD_DIM = 128
MIX_WIDTH = D_MODEL
DIFF_WIDTH = MIX_WIDTH // 2
FOX_WIDTH = MIX_WIDTH - DIFF_WIDTH
DIFF_HEADS = DIFF_WIDTH // (2 * HEAD_DIM)
FOX_HEADS = FOX_WIDTH // HEAD_DIM
IN_COLS = 3 * DIFF_WIDTH + 3 * FOX_WIDTH + FOX_HEADS
ROPE_THETA = 10000.0
NORM_EPS = 1e-5
NEG_INF = -1e30
N_EXPERTS = 32
TOP_K = 4
D_EXPERT = D_MODEL
SWIGLU_LIMIT = 7.0
SWIGLU_ALPHA = 1.702
EXPERT_BLOCK = 256

kernel_name = "hybrid_diff_fox_moe_block"


def _lambda_init(layer_idx):
    return 0.8 - 0.6 * math.exp(-0.3 * layer_idx)


def _rms_norm(x, gain):
    xf = x.astype(jnp.float32)
    y = xf * lax.rsqrt(jnp.mean(xf * xf, axis=-1, keepdims=True) + NORM_EPS)
    return (y * gain.astype(jnp.float32)).astype(x.dtype)


def _rope_tables(positions):
    inv_freq = ROPE_THETA ** (-jnp.arange(0, HEAD_DIM, 2, dtype=jnp.float32) / HEAD_DIM)
    ang = positions.astype(jnp.float32)[..., None] * inv_freq
    ang = jnp.concatenate([ang, ang], axis=-1)[:, :, None, :]
    return jnp.cos(ang), jnp.sin(ang)


def _apply_rope(x, cos, sin):
    xf = x.astype(jnp.float32)
    x1, x2 = jnp.split(xf, 2, axis=-1)
    rot = jnp.concatenate([-x2, x1], axis=-1)
    return (xf * cos + rot * sin).astype(x.dtype)


def _sweep_queries(block_fn, *per_query):
    n_qb = per_query[0].shape[1] // Q_BLOCK

    def split(a):
        a = a.reshape(a.shape[0], n_qb, Q_BLOCK, *a.shape[2:])
        return jnp.moveaxis(a, 1, 0)

    out = lax.map(block_fn, (jnp.arange(n_qb), *[split(a) for a in per_query]))
    out = jnp.moveaxis(out, 0, 1)
    return out.reshape(out.shape[0], n_qb * Q_BLOCK, *out.shape[3:])


def _diff_attention(q1, q2, k1, k2, v, lam):
    seq = k1.shape[1]
    scale = HEAD_DIM ** -0.5
    key_chunk = jnp.arange(seq) // CHUNK

    def block(args):
        i, q1b, q2b = args
        q_chunk = (i * Q_BLOCK + jnp.arange(Q_BLOCK)) // CHUNK
        mask = key_chunk[None, :] <= q_chunk[:, None]

        def probs(qb, k):
            s = jnp.einsum('bqhd,bkhd->bhqk', qb, k, preferred_element_type=jnp.float32) * scale
            return jax.nn.softmax(jnp.where(mask, s, NEG_INF), axis=-1)

        w = probs(q1b, k1) - lam * probs(q2b, k2)
        return jnp.einsum('bhqk,bkhv->bqhv', w.astype(v.dtype), v,
                          preferred_element_type=jnp.float32).astype(v.dtype)

    return _sweep_queries(block, q1, q2)


def _forgetting_attention(q, k, v, cum_log_f):
    seq = k.shape[1]
    scale = HEAD_DIM ** -0.5
    key_pos = jnp.arange(seq)
    cum_keys = jnp.transpose(cum_log_f, (0, 2, 1))

    def block(args):
        i, qb, cb = args
        q_pos = i * Q_BLOCK + jnp.arange(Q_BLOCK)
        mask = key_pos[None, :] <= q_pos[:, None]
        s = jnp.einsum('bqhd,bkhd->bhqk', qb, k, preferred_element_type=jnp.float32) * scale
        s = s + jnp.transpose(cb, (0, 2, 1))[..., :, None] - cum_keys[..., None, :]
        p = jax.nn.softmax(jnp.where(mask, s, NEG_INF), axis=-1)
        return jnp.einsum('bhqk,bkhv->bqhv', p.astype(v.dtype), v,
                          preferred_element_type=jnp.float32).astype(v.dtype)

    return _sweep_queries(block, q, cum_log_f)


def _moe(xt, w_router, b_router, w_gate, b_gate, w_up, b_up, w_down, b_down):
    n_tok, d = xt.shape
    logits = jnp.dot(xt, w_router, preferred_element_type=jnp.float32) + b_router.astype(jnp.float32)
    top_vals, top_idx = lax.top_k(logits, TOP_K)
    gates = jax.nn.softmax(top_vals, axis=-1)

    n_assign = n_tok * TOP_K
    e_flat = top_idx.reshape(-1)
    tok_flat = jnp.repeat(jnp.arange(n_tok, dtype=jnp.int32), TOP_K)
    g_flat = gates.reshape(-1)
    order = jnp.argsort(e_flat)
    e_sorted = e_flat[order]
    tok_sorted = tok_flat[order]
    g_sorted = g_flat[order]

    counts = jnp.bincount(e_flat, length=N_EXPERTS)
    starts = jnp.cumsum(counts) - counts
    padded = (counts + EXPERT_BLOCK - 1) // EXPERT_BLOCK * EXPERT_BLOCK
    pad_ends = jnp.cumsum(padded)
    pad_starts = pad_ends - padded
    dest = pad_starts[e_sorted] + (jnp.arange(n_assign) - starts[e_sorted])

    n_blocks = -(-n_assign // EXPERT_BLOCK) + N_EXPERTS
    rows = n_blocks * EXPERT_BLOCK
    xs = jnp.zeros((rows, d), xt.dtype).at[dest].set(xt[tok_sorted])
    block_expert = jnp.minimum(
        jnp.searchsorted(pad_ends, jnp.arange(n_blocks) * EXPERT_BLOCK, side='right'), N_EXPERTS - 1)

    def expert_block(args):
        xb, e = args
        g = xb @ w_gate[e] + b_gate[e]
        u = xb @ w_up[e] + b_up[e]
        g = jnp.minimum(g, SWIGLU_LIMIT)
        u = jnp.clip(u, -SWIGLU_LIMIT, SWIGLU_LIMIT)
        hdn = g * jax.nn.sigmoid(SWIGLU_ALPHA * g) * (u + 1.0)
        return hdn @ w_down[e] + b_down[e]

    ys = lax.map(expert_block, (xs.reshape(n_blocks, EXPERT_BLOCK, d), block_expert)).reshape(rows, d)
    y_assign = ys[dest] * g_sorted[:, None].astype(ys.dtype)
    return jax.ops.segment_sum(y_assign, tok_sorted, num_segments=n_tok)


def setup_inputs(seed: int = 0) -> dict:
    key = jax.random.key(seed)
    ks = jax.random.split(key, 24)
    f32 = jnp.float32

    def nrm(k, shape, scale):
        return jax.random.normal(k, shape, f32) * scale

    def gain(k, shape):
        return 1.0 + 0.02 * jax.random.normal(k, shape, f32)

    x = jax.random.normal(ks[0], (BATCH, SEQ, D_MODEL), f32)
    offset = jax.random.randint(ks[1], (BATCH, 1), 0, 1024, dtype=jnp.int32)
    positions = (offset + jnp.arange(SEQ, dtype=jnp.int32)[None, :]).astype(jnp.int32)
    return {
        'x': x,
        'positions': positions,
        'g_attn_norm': gain(ks[2], (DEPTH, D_MODEL)),
        'w_in': nrm(ks[3], (DEPTH, D_MODEL, IN_COLS), D_MODEL ** -0.5),
        'b_forget': 1.0 + 0.1 * jax.random.normal(ks[4], (DEPTH, FOX_HEADS), f32),
        'lambda_q1': nrm(ks[5], (DEPTH, HEAD_DIM), 0.1),
        'lambda_k1': nrm(ks[6], (DEPTH, HEAD_DIM), 0.1),
        'lambda_q2': nrm(ks[7], (DEPTH, HEAD_DIM), 0.1),
        'lambda_k2': nrm(ks[8], (DEPTH, HEAD_DIM), 0.1),
        'g_diff_sub': gain(ks[9], (DEPTH, 2 * HEAD_DIM)),
        'g_fox_out': gain(ks[10], (DEPTH, HEAD_DIM)),
        'w_out': nrm(ks[11], (DEPTH, MIX_WIDTH, D_MODEL), MIX_WIDTH ** -0.5),
        'g_ffn_norm': gain(ks[12], (DEPTH, D_MODEL)),
        'w_router': nrm(ks[13], (DEPTH, D_MODEL, N_EXPERTS), D_MODEL ** -0.5),
        'b_router': nrm(ks[14], (DEPTH, N_EXPERTS), 0.01),
        'w_gate': nrm(ks[15], (DEPTH, N_EXPERTS, D_MODEL, D_EXPERT), D_MODEL ** -0.5),
        'b_gate': nrm(ks[16], (DEPTH, N_EXPERTS, D_EXPERT), 0.01),
        'w_up': nrm(ks[17], (DEPTH, N_EXPERTS, D_MODEL, D_EXPERT), D_MODEL ** -0.5),
        'b_up': nrm(ks[18], (DEPTH, N_EXPERTS, D_EXPERT), 0.01),
        'w_down': nrm(ks[19], (DEPTH, N_EXPERTS, D_EXPERT, D_MODEL), D_EXPERT ** -0.5),
        'b_down': nrm(ks[20], (DEPTH, N_EXPERTS, D_MODEL), 0.01),
        'g_final': gain(ks[21], (D_MODEL,)),
    }


def reference(x, positions, g_attn_norm, w_in, b_forget, lambda_q1, lambda_k1, lambda_q2, lambda_k2,
              g_diff_sub, g_fox_out, w_out, g_ffn_norm, w_router, b_router, w_gate, b_gate,
              w_up, b_up, w_down, b_down, g_final):
    bsz, seq, _ = x.shape
    cos, sin = _rope_tables(positions)
    split_at = [DIFF_WIDTH * i for i in range(1, 4)] + [3 * DIFF_WIDTH + FOX_WIDTH * i for i in range(1, 4)]

    for l in range(DEPTH):
        lam_init = _lambda_init(l)
        xn = _rms_norm(x, g_attn_norm[l])
        proj = xn @ w_in[l]
        dq, dk, dv, fq, fk, fv, f_logit = jnp.split(proj, split_at, axis=-1)

        dq = dq.reshape(bsz, seq, DIFF_HEADS, 2, HEAD_DIM)
        dk = dk.reshape(bsz, seq, DIFF_HEADS, 2, HEAD_DIM)
        q1 = _apply_rope(dq[..., 0, :], cos, sin)
        q2 = _apply_rope(dq[..., 1, :], cos, sin)
        k1 = _apply_rope(dk[..., 0, :], cos, sin)
        k2 = _apply_rope(dk[..., 1, :], cos, sin)
        dv = dv.reshape(bsz, seq, DIFF_HEADS, 2 * HEAD_DIM)
        lam = (jnp.exp(jnp.sum(lambda_q1[l].astype(jnp.float32) * lambda_k1[l].astype(jnp.float32)))
               - jnp.exp(jnp.sum(lambda_q2[l].astype(jnp.float32) * lambda_k2[l].astype(jnp.float32)))
               + lam_init)
        od = _diff_attention(q1, q2, k1, k2, dv, lam)
        od = _rms_norm(od, g_diff_sub[l]) * (1.0 - lam_init)

        fq = fq.reshape(bsz, seq, FOX_HEADS, HEAD_DIM)
        fk = fk.reshape(bsz, seq, FOX_HEADS, HEAD_DIM)
        fv = fv.reshape(bsz, seq, FOX_HEADS, HEAD_DIM)
        log_f = jax.nn.log_sigmoid(f_logit.astype(jnp.float32) + b_forget[l].astype(jnp.float32))
        cum_log_f = jnp.cumsum(log_f, axis=1)
        of = _forgetting_attention(fq, fk, fv, cum_log_f)
        of = _rms_norm(of, g_fox_out[l])

        mix = jnp.concatenate([od.reshape(bsz, seq, DIFF_WIDTH), of.reshape(bsz, seq, FOX_WIDTH)], axis=-1)
        h = x + mix @ w_out[l]

        hn = _rms_norm(h, g_ffn_norm[l]).reshape(bsz * seq, D_MODEL)
        moe_out = _moe(hn, w_router[l], b_router[l], w_gate[l], b_gate[l], w_up[l], b_up[l],
                       w_down[l], b_down[l])
        x = h + moe_out.reshape(bsz, seq, D_MODEL).astype(h.dtype)

    return _rms_norm(x, g_final)
```

```python
import functools
import math

import jax
import jax.numpy as jnp
from jax import lax
from jax.experimental import pallas as pl
from jax.experimental.pallas import tpu as pltpu

F32 = jnp.float32
BF16 = jnp.bfloat16
HIGHEST = lax.Precision.HIGHEST

HEAD_DIM = 128
CHUNK = 64
ROPE_THETA = 10000.0
NORM_EPS = 1e-5
NEG_BIG = -1e30
TOP_K = 4
SWIGLU_LIMIT = 7.0
SWIGLU_ALPHA = 1.702
LANES = 128
QK_SCALE = HEAD_DIM ** -0.5

INPROJ_TM = 512
INPROJ_TN = 512
ATTN_T = 256
OUTPROJ_TM = 256
ROUTE_TM = 512
DISPATCH_TM = 256
EXPERT_TM = 512
EXPERT_TC = 512
COMBINE_TM = 128
VMEM_LIMIT = 48 * 1024 * 1024


def _lambda_init(layer_idx):
    return 0.8 - 0.6 * math.exp(-0.3 * layer_idx)


def _cparams(sem):
    return pltpu.CompilerParams(dimension_semantics=sem, vmem_limit_bytes=VMEM_LIMIT)


def _inproj_body(x_ref, g_ref, w_ref, wf_ref, cos_ref, sin_ref, o_ref, fl_ref, xn_ref,
                 *, n_rope, tn, q_ranges):
    j = pl.program_id(1)

    @pl.when(j == 0)
    def _():
        x = x_ref[...]
        ms = jnp.mean(x * x, axis=-1, keepdims=True)
        xn = x * lax.rsqrt(ms + NORM_EPS) * g_ref[...]
        xn_ref[...] = xn.astype(BF16)
        fl_ref[...] = jnp.dot(xn, wf_ref[...], precision=HIGHEST, preferred_element_type=F32)

    acc = jnp.dot(xn_ref[...], w_ref[...], preferred_element_type=F32)
    col0 = j * tn
    is_q = (col0 >= q_ranges[0][0]) & (col0 < q_ranges[0][1])
    is_q = is_q | ((col0 >= q_ranges[1][0]) & (col0 < q_ranges[1][1]))
    scale = jnp.where(is_q, QK_SCALE, 1.0).astype(F32)

    @pl.when(j < n_rope)
    def _():
        cos = cos_ref[...] * scale
        sin = sin_ref[...] * scale
        for gi in range(tn // LANES):
            xg = acc[:, gi * LANES:(gi + 1) * LANES]
            rot = pltpu.roll(xg, HEAD_DIM // 2, 1)
            o_ref[:, gi * LANES:(gi + 1) * LANES] = (xg * cos + rot * sin).astype(BF16)

    @pl.when(j >= n_rope)
    def _():
        o_ref[...] = (acc * scale).astype(BF16)


def _inproj(x2, g, w_main, w_f, cos, sin_signed, *, diff_w, fox_w):
    n, d = x2.shape
    cols = w_main.shape[1]
    tm, tn = INPROJ_TM, INPROJ_TN
    n_rope = 2 * diff_w // tn
    q_ranges = ((0, diff_w), (3 * diff_w, 3 * diff_w + fox_w))
    body = functools.partial(_inproj_body, n_rope=n_rope, tn=tn, q_ranges=q_ranges)
    return pl.pallas_call(
        body,
        out_shape=(jax.ShapeDtypeStruct((n, cols), BF16), jax.ShapeDtypeStruct((n, LANES), F32)),
        grid=(n // tm, cols // tn),
        in_specs=[
            pl.BlockSpec((tm, d), lambda i, j: (i, 0)),
            pl.BlockSpec((1, d), lambda i, j: (0, 0)),
            pl.BlockSpec((d, tn), lambda i, j: (0, j)),
            pl.BlockSpec((d, LANES), lambda i, j: (0, 0)),
            pl.BlockSpec((tm, LANES), lambda i, j: (i, 0)),
            pl.BlockSpec((tm, LANES), lambda i, j: (i, 0)),
        ],
        out_specs=(pl.BlockSpec((tm, tn), lambda i, j: (i, j)),
                   pl.BlockSpec((tm, LANES), lambda i, j: (i, 0))),
        scratch_shapes=[pltpu.VMEM((tm, d), BF16)],
        compiler_params=_cparams(("parallel", "arbitrary")),
        name="inproj",
    )(x2, g, w_main, w_f, cos, sin_signed)


def _cumgate_body(z_ref, b_ref, c_ref):
    z = z_ref[...] + b_ref[...]
    v = jnp.minimum(z, 0.0) - jnp.log(1.0 + jnp.exp(-jnp.abs(z)))
    seq = v.shape[1]
    lane = lax.broadcasted_iota(jnp.int32, v.shape, 1)
    shift = 1
    while shift < seq:
        v = v + jnp.where(lane >= shift, pltpu.roll(v, shift, 1), 0.0)
        shift *= 2
    c_ref[...] = v


def _cumgate(z_rows, b_rows):
    return pl.pallas_call(
        _cumgate_body,
        out_shape=jax.ShapeDtypeStruct(z_rows.shape, F32),
        name="cumgate",
    )(z_rows, b_rows)


def _online_softmax_step(s, v, m_ref, l_ref, a_ref):
    m_prev = m_ref[...]
    m_new = jnp.maximum(m_prev, jnp.max(s, axis=-1, keepdims=True))
    alpha = jnp.exp(m_prev - m_new)
    p = jnp.exp(s - m_new)
    l_ref[...] = alpha * l_ref[...] + jnp.sum(p, axis=-1, keepdims=True)
    a_ref[...] = alpha * a_ref[...] + jnp.dot(p.astype(BF16), v, preferred_element_type=F32)
    m_ref[...] = m_new


def _qk(q, k):
    return lax.dot_general(q, k, (((1,), (1,)), ((), ())), preferred_element_type=F32)


def _diff_body(lq1_ref, lk1_ref, lq2_ref, lk2_ref, gsub_ref, q1_ref, q2_ref, k1_ref, k2_ref, v_ref,
               o_ref, m1, l1, a1, m2, l2, a2, *, t, lam_init):
    qi = pl.program_id(2)
    for m_ref, l_ref, a_ref in ((m1, l1, a1), (m2, l2, a2)):
        m_ref[...] = jnp.full(m_ref.shape, NEG_BIG, F32)
        l_ref[...] = jnp.zeros(l_ref.shape, F32)
        a_ref[...] = jnp.zeros(a_ref.shape, F32)
    q1 = q1_ref[...]
    q2 = q2_ref[...]

    def step(ki, mask):
        off = pl.multiple_of(ki * t, t)
        v = v_ref[pl.ds(off, t), :]
        for q, k_ref, m_ref, l_ref, a_ref in ((q1, k1_ref, m1, l1, a1), (q2, k2_ref, m2, l2, a2)):
            s = _qk(q, k_ref[pl.ds(off, t), :])
            if mask is not None:
                s = jnp.where(mask, s, NEG_BIG)
            _online_softmax_step(s, v, m_ref, l_ref, a_ref)

    def loop_body(ki, carry):
        step(ki, None)
        return carry

    lax.fori_loop(0, qi, loop_body, 0)
    row_chunk = lax.broadcasted_iota(jnp.int32, (t, t), 0) // CHUNK
    col_chunk = lax.broadcasted_iota(jnp.int32, (t, t), 1) // CHUNK
    step(qi, col_chunk <= row_chunk)

    lam = (jnp.exp(jnp.sum(lq1_ref[...] * lk1_ref[...], axis=-1, keepdims=True))
           - jnp.exp(jnp.sum(lq2_ref[...] * lk2_ref[...], axis=-1, keepdims=True)) + lam_init)
    o = a1[...] * (1.0 / l1[...]) - lam * (a2[...] * (1.0 / l2[...]))
    ms = jnp.mean(o * o, axis=-1, keepdims=True)
    y = o * lax.rsqrt(ms + NORM_EPS) * gsub_ref[...]
    o_ref[...] = (y * (1.0 - lam_init)).astype(BF16)


def _diff_attention(proj, lq1, lk1, lq2, lk2, gsub, *, bsz, seq, diff_w, lam_init):
    t = ATTN_T
    n_heads = diff_w // (2 * HEAD_DIM)
    nq = seq // t
    kcol = diff_w // HEAD_DIM
    vcol = 2 * diff_w // (2 * HEAD_DIM)
    vec = pl.BlockSpec((1, HEAD_DIM), lambda b, h, i: (0, 0))
    body = functools.partial(_diff_body, t=t, lam_init=lam_init)
    return pl.pallas_call(
        body,
        out_shape=jax.ShapeDtypeStruct((bsz * seq, diff_w), BF16),
        grid=(bsz, n_heads, nq),
        in_specs=[
            vec, vec, vec, vec,
            pl.BlockSpec((1, 2 * HEAD_DIM), lambda b, h, i: (0, 0)),
            pl.BlockSpec((t, HEAD_DIM), lambda b, h, i: (b * nq + i, 2 * h)),
            pl.BlockSpec((t, HEAD_DIM), lambda b, h, i: (b * nq + i, 2 * h + 1)),
            pl.BlockSpec((seq, HEAD_DIM), lambda b, h, i: (b, kcol + 2 * h)),
            pl.BlockSpec((seq, HEAD_DIM), lambda b, h, i: (b, kcol + 2 * h + 1)),
            pl.BlockSpec((seq, 2 * HEAD_DIM), lambda b, h, i: (b, vcol + h)),
        ],
        out_specs=pl.BlockSpec((t, 2 * HEAD_DIM), lambda b, h, i: (b * nq + i, h)),
        scratch_shapes=[pltpu.VMEM((t, 1), F32), pltpu.VMEM((t, 1), F32), pltpu.VMEM((t, 2 * HEAD_DIM), F32),
                        pltpu.VMEM((t, 1), F32), pltpu.VMEM((t, 1), F32), pltpu.VMEM((t, 2 * HEAD_DIM), F32)],
        compiler_params=_cparams(("parallel", "parallel", "arbitrary")),
        name="diffattn",
    )(lq1, lk1, lq2, lk2, gsub, proj, proj, proj, proj, proj)


def _fox_body(g_ref, q_ref, k_ref, v_ref, ccol_ref, crow_ref, o_ref, m, l, a, *, t):
    qi = pl.program_id(2)
    m[...] = jnp.full(m.shape, NEG_BIG, F32)
    l[...] = jnp.zeros(l.shape, F32)
    a[...] = jnp.zeros(a.shape, F32)
    q = q_ref[...]
    cq = ccol_ref[...]

    def step(ki, mask):
        off = pl.multiple_of(ki * t, t)
        s = _qk(q, k_ref[pl.ds(off, t), :]) + (cq - crow_ref[:, pl.ds(off, t)])
        if mask is not None:
            s = jnp.where(mask, s, NEG_BIG)
        _online_softmax_step(s, v_ref[pl.ds(off, t), :], m, l, a)

    def loop_body(ki, carry):
        step(ki, None)
        return carry

    lax.fori_loop(0, qi, loop_body, 0)
    row = lax.broadcasted_iota(jnp.int32, (t, t), 0)
    col = lax.broadcasted_iota(jnp.int32, (t, t), 1)
    step(qi, col <= row)

    o = a[...] * (1.0 / l[...])
    ms = jnp.mean(o * o, axis=-1, keepdims=True)
    o_ref[...] = (o * lax.rsqrt(ms + NORM_EPS) * g_ref[...]).astype(BF16)


def _fox_attention(proj, c_cols, c_rows, g_fox, *, bsz, seq, diff_w, fox_w):
    t = ATTN_T
    n_heads = fox_w // HEAD_DIM
    nq = seq // t
    qcol = 3 * diff_w // HEAD_DIM
    kcol = qcol + n_heads
    vcol = kcol + n_heads
    body = functools.partial(_fox_body, t=t)
    return pl.pallas_call(
        body,
        out_shape=jax.ShapeDtypeStruct((bsz * seq, fox_w), BF16),
        grid=(bsz, n_heads, nq),
        in_specs=[
            pl.BlockSpec((1, HEAD_DIM), lambda b, h, i: (0, 0)),
            pl.BlockSpec((t, HEAD_DIM), lambda b, h, i: (b * nq + i, qcol + h)),
            pl.BlockSpec((seq, HEAD_DIM), lambda b, h, i: (b, kcol + h)),
            pl.BlockSpec((seq, HEAD_DIM), lambda b, h, i: (b, vcol + h)),
            pl.BlockSpec((None, t, 1), lambda b, h, i: (b * n_heads + h, i, 0)),
            pl.BlockSpec((None, 1, seq), lambda b, h, i: (b * n_heads + h, 0, 0)),
        ],
        out_specs=pl.BlockSpec((t, HEAD_DIM), lambda b, h, i: (b * nq + i, h)),
        scratch_shapes=[pltpu.VMEM((t, 1), F32), pltpu.VMEM((t, 1), F32), pltpu.VMEM((t, HEAD_DIM), F32)],
        compiler_params=_cparams(("parallel", "parallel", "arbitrary")),
        name="foxattn",
    )(g_fox, proj, proj, proj, c_cols, c_rows)


def _outproj_body(od_ref, of_ref, wd_ref, wf_ref, x_ref, g_ref, wr_ref, br_ref, h_ref, hn_ref, lg_ref):
    h = (x_ref[...]
         + jnp.dot(od_ref[...], wd_ref[...], preferred_element_type=F32)
         + jnp.dot(of_ref[...], wf_ref[...], preferred_element_type=F32))
    h_ref[...] = h
    ms = jnp.mean(h * h, axis=-1, keepdims=True)
    hn = h * lax.rsqrt(ms + NORM_EPS) * g_ref[...]
    hn_ref[...] = hn
    lg_ref[...] = jnp.dot(hn, wr_ref[...], precision=HIGHEST, preferred_element_type=F32) + br_ref[...]


def _outproj(od, of, w_od, w_of, x2, g_ffn, w_router_p, b_router_p):
    n, d = x2.shape
    tm = OUTPROJ_TM
    row = lambda i: (i, 0)
    const = lambda i: (0, 0)
    return pl.pallas_call(
        _outproj_body,
        out_shape=(jax.ShapeDtypeStruct((n, d), F32), jax.ShapeDtypeStruct((n, d), F32),
                   jax.ShapeDtypeStruct((n, LANES), F32)),
        grid=(n // tm,),
        in_specs=[
            pl.BlockSpec((tm, od.shape[1]), row),
            pl.BlockSpec((tm, of.shape[1]), row),
            pl.BlockSpec(w_od.shape, const),
            pl.BlockSpec(w_of.shape, const),
            pl.BlockSpec((tm, d), row),
            pl.BlockSpec((1, d), const),
            pl.BlockSpec((d, LANES), const),
            pl.BlockSpec((1, LANES), const),
        ],
        out_specs=(pl.BlockSpec((tm, d), row), pl.BlockSpec((tm, d), row), pl.BlockSpec((tm, LANES), row)),
        compiler_params=_cparams(("parallel",)),
        name="outproj",
    )(od, of, w_od, w_of, x2, g_ffn, w_router_p, b_router_p)


def _route_body(lg_ref, e_ref, gate_ref, rank_ref, cnt_ref, carry_ref, *, n_experts):
    i = pl.program_id(0)
    tm = lg_ref.shape[0]

    @pl.when(i == 0)
    def _():
        carry_ref[...] = jnp.zeros(carry_ref.shape, F32)

    lane = lax.broadcasted_iota(jnp.int32, (tm, LANES), 1)
    logits = jnp.where(lane < n_experts, lg_ref[...], -jnp.inf)
    vals, idxs, hots = [], [], []
    for _ in range(TOP_K):
        mx = jnp.max(logits, axis=-1, keepdims=True)
        idx = jnp.min(jnp.where(logits == mx, lane, LANES), axis=-1, keepdims=True)
        hot = lane == idx
        logits = jnp.where(hot, -jnp.inf, logits)
        vals.append(mx)
        idxs.append(idx)
        hots.append(hot)

    exps = [jnp.exp(v - vals[0]) for v in vals]
    denom = exps[0]
    for ex in exps[1:]:
        denom = denom + ex
    inv = 1.0 / denom

    hot_sum = jnp.zeros((tm, LANES), F32)
    for hot in hots:
        hot_sum = hot_sum + jnp.where(hot, 1.0, 0.0)
    r = lax.broadcasted_iota(jnp.int32, (tm, tm), 0)
    c = lax.broadcasted_iota(jnp.int32, (tm, tm), 1)
    earlier = jnp.where(c < r, 1.0, 0.0).astype(BF16)
    base = carry_ref[...] + jnp.dot(earlier, hot_sum.astype(BF16), preferred_element_type=F32)

    e_out = jnp.zeros((tm, LANES), jnp.int32)
    g_out = jnp.zeros((tm, LANES), F32)
    r_out = jnp.zeros((tm, LANES), jnp.int32)
    for k in range(TOP_K):
        rank = jnp.sum(jnp.where(hots[k], base, 0.0), axis=-1, keepdims=True).astype(jnp.int32)
        e_out = jnp.where(lane == k, idxs[k], e_out)
        g_out = jnp.where(lane == k, exps[k] * inv, g_out)
        r_out = jnp.where(lane == k, rank, r_out)
    e_ref[...] = e_out
    gate_ref[...] = g_out
    rank_ref[...] = r_out
    carry_ref[...] = carry_ref[...] + jnp.sum(hot_sum, axis=0, keepdims=True)
    cnt_ref[...] = carry_ref[...]


def _route(logits, n_experts):
    n = logits.shape[0]
    tm = ROUTE_TM
    row = lambda i: (i, 0)
    body = functools.partial(_route_body, n_experts=n_experts)
    return pl.pallas_call(
        body,
        out_shape=(jax.ShapeDtypeStruct((n, LANES), jnp.int32), jax.ShapeDtypeStruct((n, LANES), F32),
                   jax.ShapeDtypeStruct((n, LANES), jnp.int32), jax.ShapeDtypeStruct((1, LANES), F32)),
        grid=(n // tm,),
        in_specs=[pl.BlockSpec((tm, LANES), row)],
        out_specs=(pl.BlockSpec((tm, LANES), row), pl.BlockSpec((tm, LANES), row),
                   pl.BlockSpec((tm, LANES), row), pl.BlockSpec((1, LANES), lambda i: (0, 0))),
        scratch_shapes=[pltpu.VMEM((1, LANES), F32)],
        compiler_params=_cparams(("arbitrary",)),
        name="route",
    )(logits)


def _dispatch_body(dest_ref, hn_ref, xs_in_ref, xs_ref, sem):
    del xs_in_ref
    tm = hn_ref.shape[0]

    def row_copy(a):
        t = a // TOP_K
        return pltpu.make_async_copy(hn_ref.at[pl.ds(t, 1), :], xs_ref.at[pl.ds(dest_ref[0, a], 1), :], sem)

    def start(a, carry):
        row_copy(a).start()
        return carry

    def wait(a, carry):
        row_copy(a).wait()
        return carry

    lax.fori_loop(0, tm * TOP_K, start, 0)
    lax.fori_loop(0, tm * TOP_K, wait, 0)


def _dispatch(dest, hn, xs_init):
    n, d = hn.shape
    tm = DISPATCH_TM
    dest3 = dest.reshape(n // tm, 1, tm * TOP_K)
    return pl.pallas_call(
        _dispatch_body,
        out_shape=jax.ShapeDtypeStruct(xs_init.shape, xs_init.dtype),
        grid=(n // tm,),
        in_specs=[
            pl.BlockSpec((None, 1, tm * TOP_K), lambda i: (i, 0, 0), memory_space=pltpu.SMEM),
            pl.BlockSpec((tm, d), lambda i: (i, 0)),
            pl.BlockSpec(memory_space=pl.ANY),
        ],
        out_specs=pl.BlockSpec(memory_space=pl.ANY),
        scratch_shapes=[pltpu.SemaphoreType.DMA],
        input_output_aliases={2: 0},
        compiler_params=_cparams(("arbitrary",)),
        name="dispatch",
    )(dest3, hn, xs_init)


def _experts_body(te_ref, nu_ref, x_ref, wg_ref, bg_ref, wu_ref, bu_ref, wd_ref, bd_ref, o_ref, xb_ref):
    i = pl.program_id(0)
    j = pl.program_id(1)

    @pl.when(i < nu_ref[0])
    def _():
        @pl.when(j == 0)
        def _():
            xb_ref[...] = x_ref[...].astype(BF16)
            o_ref[...] = jnp.broadcast_to(bd_ref[...], o_ref.shape)

        xb = xb_ref[...]
        g = jnp.dot(xb, wg_ref[...], preferred_element_type=F32) + bg_ref[...]
        u = jnp.dot(xb, wu_ref[...], preferred_element_type=F32) + bu_ref[...]
        g = jnp.minimum(g, SWIGLU_LIMIT)
        u = jnp.clip(u, -SWIGLU_LIMIT, SWIGLU_LIMIT)
        hdn = g * (1.0 / (1.0 + jnp.exp(-SWIGLU_ALPHA * g))) * (u + 1.0)
        o_ref[...] += jnp.dot(hdn.astype(BF16), wd_ref[...], preferred_element_type=F32)

    @pl.when((i >= nu_ref[0]) & (j == 0))
    def _():
        o_ref[...] = jnp.zeros(o_ref.shape, F32)


def _experts(tile_expert, n_used, xs, wg, bg, wu, bu, wd, bd):
    rows, d = xs.shape
    d_exp = wg.shape[2]
    tm, tc = EXPERT_TM, EXPERT_TC
    n_tiles = rows // tm
    nj = d_exp // tc

    def tile(i, nu):
        return jnp.minimum(i, nu[0] - 1)

    def chunk(i, j, nu):
        return jnp.where(i < nu[0], j, nj - 1)

    return pl.pallas_call(
        _experts_body,
        out_shape=jax.ShapeDtypeStruct((rows, d), F32),
        grid_spec=pltpu.PrefetchScalarGridSpec(
            num_scalar_prefetch=2,
            grid=(n_tiles, nj),
            in_specs=[
                pl.BlockSpec((tm, d), lambda i, j, te, nu: (tile(i, nu), 0)),
                pl.BlockSpec((None, d, tc), lambda i, j, te, nu: (te[i], 0, chunk(i, j, nu))),
                pl.BlockSpec((None, 1, tc), lambda i, j, te, nu: (te[i], 0, chunk(i, j, nu))),
                pl.BlockSpec((None, d, tc), lambda i, j, te, nu: (te[i], 0, chunk(i, j, nu))),
                pl.BlockSpec((None, 1, tc), lambda i, j, te, nu: (te[i], 0, chunk(i, j, nu))),
                pl.BlockSpec((None, tc, d), lambda i, j, te, nu: (te[i], chunk(i, j, nu), 0)),
                pl.BlockSpec((None, 1, d), lambda i, j, te, nu: (te[i], 0, 0)),
            ],
            out_specs=pl.BlockSpec((tm, d), lambda i, j, te, nu: (i, 0)),
            scratch_shapes=[pltpu.VMEM((tm, d), BF16)],
        ),
        compiler_params=_cparams(("arbitrary", "arbitrary")),
        name="experts",
    )(tile_expert, n_used, xs, wg, bg, wu, bu, wd, bd)


def _combine_body(dcur_ref, dnext_ref, gate_ref, h_ref, gfin_ref, ys_ref, o_ref, buf, sems, *, final_norm):
    i = pl.program_id(0)
    nb = pl.num_programs(0)
    tm = h_ref.shape[0]
    slot = i % 2

    def row_copy(dref, s, a):
        t = a // TOP_K
        k = a % TOP_K
        return pltpu.make_async_copy(ys_ref.at[pl.ds(dref[0, a], 1), :], buf.at[s, k, pl.ds(t, 1), :], sems.at[s])

    def start_all(dref, s):
        def go(a, carry):
            row_copy(dref, s, a).start()
            return carry
        lax.fori_loop(0, tm * TOP_K, go, 0)

    @pl.when(i == 0)
    def _():
        start_all(dcur_ref, 0)

    @pl.when(i + 1 < nb)
    def _():
        start_all(dnext_ref, 1 - slot)

    def wait(a, carry):
        row_copy(dcur_ref, slot, a).wait()
        return carry
    lax.fori_loop(0, tm * TOP_K, wait, 0)

    y = h_ref[...]
    gates = gate_ref[...]
    for k in range(TOP_K):
        y = y + gates[:, k:k + 1] * buf[slot, k]
    if final_norm:
        ms = jnp.mean(y * y, axis=-1, keepdims=True)
        y = y * lax.rsqrt(ms + NORM_EPS) * gfin_ref[...]
    o_ref[...] = y


def _combine(dest, gates, h, g_final, ys, *, final_norm):
    n, d = h.shape
    tm = COMBINE_TM
    nb = n // tm
    dest3 = dest.reshape(nb, 1, tm * TOP_K)
    smem_blk = (None, 1, tm * TOP_K)
    return pl.pallas_call(
        functools.partial(_combine_body, final_norm=final_norm),
        out_shape=jax.ShapeDtypeStruct((n, d), F32),
        grid=(nb,),
        in_specs=[
            pl.BlockSpec(smem_blk, lambda i: (i, 0, 0), memory_space=pltpu.SMEM),
            pl.BlockSpec(smem_blk, lambda i: (jnp.minimum(i + 1, nb - 1), 0, 0), memory_space=pltpu.SMEM),
            pl.BlockSpec((tm, LANES), lambda i: (i, 0)),
            pl.BlockSpec((tm, d), lambda i: (i, 0)),
            pl.BlockSpec((1, d), lambda i: (0, 0)),
            pl.BlockSpec(memory_space=pl.ANY),
        ],
        out_specs=pl.BlockSpec((tm, d), lambda i: (i, 0)),
        scratch_shapes=[pltpu.VMEM((2, TOP_K, tm, d), F32), pltpu.SemaphoreType.DMA((2,))],
        compiler_params=_cparams(("arbitrary",)),
        name="combine",
    )(dest3, dest3, gates, h, g_final, ys)


def _rope_tables(positions):
    inv_freq = ROPE_THETA ** (-jnp.arange(0, HEAD_DIM, 2, dtype=F32) / HEAD_DIM)
    ang = positions.astype(F32)[..., None] * inv_freq
    ang = jnp.concatenate([ang, ang], axis=-1)
    sign = jnp.concatenate([-jnp.ones((HEAD_DIM // 2,), F32), jnp.ones((HEAD_DIM // 2,), F32)])
    return jnp.cos(ang), jnp.sin(ang) * sign


def _layer(x2, cos, sin_signed, layer_idx, bsz, seq, g_attn_norm, w_in, b_forget, lambda_q1, lambda_k1,
           lambda_q2, lambda_k2, g_diff_sub, g_fox_out, w_out, g_ffn_norm, w_router, b_router,
           w_gate, b_gate, w_up, b_up, w_down, b_down):
    n, d = x2.shape
    diff_w = d // 2
    fox_w = d - diff_w
    fox_heads = fox_w // HEAD_DIM
    main_cols = 3 * diff_w + 3 * fox_w
    n_experts = w_router.shape[1]
    lam_init = _lambda_init(layer_idx)

    w_main = w_in[:, :main_cols].astype(BF16)
    w_f = jnp.pad(w_in[:, main_cols:], ((0, 0), (0, LANES - fox_heads)))
    proj, f_logit = _inproj(x2, g_attn_norm[None, :], w_main, w_f, cos, sin_signed, diff_w=diff_w, fox_w=fox_w)

    z_rows = f_logit[:, :fox_heads].reshape(bsz, seq, fox_heads).transpose(0, 2, 1).reshape(bsz * fox_heads, seq)
    b_rows = jnp.tile(b_forget.astype(F32), bsz)[:, None]
    c_rows = _cumgate(z_rows, b_rows)

    od = _diff_attention(proj, lambda_q1[None, :], lambda_k1[None, :], lambda_q2[None, :], lambda_k2[None, :],
                         g_diff_sub[None, :], bsz=bsz, seq=seq, diff_w=diff_w, lam_init=lam_init)
    of = _fox_attention(proj, c_rows[:, :, None], c_rows[:, None, :], g_fox_out[None, :],
                        bsz=bsz, seq=seq, diff_w=diff_w, fox_w=fox_w)

    w_out_b = w_out.astype(BF16)
    w_router_p = jnp.pad(w_router, ((0, 0), (0, LANES - n_experts)))
    b_router_p = jnp.pad(b_router, (0, LANES - n_experts))[None, :]
    h, hn, logits = _outproj(od, of, w_out_b[:diff_w], w_out_b[diff_w:], x2, g_ffn_norm[None, :],
                             w_router_p, b_router_p)

    e_pad, gate_pad, rank_pad, cnt = _route(logits, n_experts)
    e_idx = e_pad[:, :TOP_K]
    counts = cnt[0, :n_experts].astype(jnp.int32)
    padded = (counts + EXPERT_TM - 1) // EXPERT_TM * EXPERT_TM
    pad_ends = jnp.cumsum(padded)
    pad_starts = pad_ends - padded
    start_of = jnp.sum(jnp.where(e_idx[..., None] == jnp.arange(n_experts), pad_starts, 0), axis=-1)
    dest = (start_of + rank_pad[:, :TOP_K]).astype(jnp.int32)
    n_tiles = -(-n * TOP_K // EXPERT_TM) + n_experts
    tile_start = jnp.arange(n_tiles, dtype=jnp.int32) * EXPERT_TM
    tile_expert = jnp.sum((pad_ends[None, :] <= tile_start[:, None]).astype(jnp.int32), axis=1)
    tile_expert = jnp.minimum(tile_expert, n_experts - 1)
    n_used = (pad_ends[-1:] // EXPERT_TM).astype(jnp.int32)
    tile_expert = jnp.where(jnp.arange(n_tiles) < n_used[0], tile_expert, tile_expert[n_used[0] - 1])

    xs = _dispatch(dest, hn, jnp.zeros((n_tiles * EXPERT_TM, d), F32))
    ys = _experts(tile_expert, n_used, xs,
                  w_gate.astype(BF16), b_gate[:, None, :], w_up.astype(BF16), b_up[:, None, :],
                  w_down.astype(BF16), b_down[:, None, :])
    return dest, gate_pad, h, ys


def kernel(x, positions, g_attn_norm, w_in, b_forget, lambda_q1, lambda_k1, lambda_q2, lambda_k2, g_diff_sub,
           g_fox_out, w_out, g_ffn_norm, w_router, b_router, w_gate, b_gate, w_up, b_up, w_down, b_down,
           g_final):
    bsz, seq, d = x.shape
    depth = w_in.shape[0]
    cos, sin_signed = _rope_tables(positions)
    cos = cos.reshape(bsz * seq, HEAD_DIM)
    sin_signed = sin_signed.reshape(bsz * seq, HEAD_DIM)
    x2 = x.reshape(bsz * seq, d)
    for l in range(depth):
        dest, gates, h, ys = _layer(
            x2, cos, sin_signed, l, bsz, seq, g_attn_norm[l], w_in[l], b_forget[l], lambda_q1[l], lambda_k1[l],
            lambda_q2[l], lambda_k2[l], g_diff_sub[l], g_fox_out[l], w_out[l], g_ffn_norm[l], w_router[l],
            b_router[l], w_gate[l], b_gate[l], w_up[l], b_up[l], w_down[l], b_down[l])
        x2 = _combine(dest, gates, h, g_final[None, :], ys, final_norm=(l == depth - 1))
    return x2.reshape(bsz, seq, d)
```

```python
import functools
import math

import jax
import jax.numpy as jnp
from jax import lax
from jax.experimental import pallas as pl
from jax.experimental.pallas import tpu as pltpu

F32 = jnp.float32
BF16 = jnp.bfloat16
HIGHEST = lax.Precision.HIGHEST

HEAD_DIM = 128
CHUNK = 64
ROPE_THETA = 10000.0
NORM_EPS = 1e-5
NEG_BIG = -1e30
TOP_K = 4
SWIGLU_LIMIT = 7.0
SWIGLU_ALPHA = 1.702
LANES = 128
QK_SCALE = HEAD_DIM ** -0.5
LOG2E = math.log2(math.e)
ONES_ROWS = 16

INPROJ_TM = 512
INPROJ_TN = 512
ATTN_T = 512
VT_CHUNK = 512
OUTPROJ_TM = 256
ROUTE_TM = 512
DISPATCH_TM = 256
EXPERT_TM = 512
EXPERT_TC = 512
COMBINE_TM = 128
VMEM_LIMIT = 48 * 1024 * 1024


def _lambda_init(layer_idx):
    return 0.8 - 0.6 * math.exp(-0.3 * layer_idx)


def _cparams(sem):
    return pltpu.CompilerParams(dimension_semantics=sem, vmem_limit_bytes=VMEM_LIMIT)


def _inproj_body(x_ref, g_ref, w_ref, wf_ref, cos_ref, sin_ref, o_ref, fl_ref, xn_ref,
                 *, n_rope, tn, q_ranges):
    j = pl.program_id(1)

    @pl.when(j == 0)
    def _():
        x = x_ref[...]
        ms = jnp.mean(x * x, axis=-1, keepdims=True)
        xn = x * lax.rsqrt(ms + NORM_EPS) * g_ref[...]
        xn_ref[...] = xn.astype(BF16)
        fl_ref[...] = jnp.dot(xn, wf_ref[...], precision=HIGHEST, preferred_element_type=F32)

    acc = jnp.dot(xn_ref[...], w_ref[...], preferred_element_type=F32)
    col0 = j * tn
    is_q = (col0 >= q_ranges[0][0]) & (col0 < q_ranges[0][1])
    is_q = is_q | ((col0 >= q_ranges[1][0]) & (col0 < q_ranges[1][1]))
    scale = jnp.where(is_q, QK_SCALE * LOG2E, 1.0).astype(F32)

    @pl.when(j < n_rope)
    def _():
        cos = cos_ref[...] * scale
        sin = sin_ref[...] * scale
        for gi in range(tn // LANES):
            xg = acc[:, gi * LANES:(gi + 1) * LANES]
            rot = pltpu.roll(xg, HEAD_DIM // 2, 1)
            o_ref[:, gi * LANES:(gi + 1) * LANES] = (xg * cos + rot * sin).astype(BF16)

    @pl.when(j >= n_rope)
    def _():
        o_ref[...] = (acc * scale).astype(BF16)


def _inproj(x2, g, w_main, w_f, cos, sin_signed, *, diff_w, fox_w):
    n, d = x2.shape
    cols = w_main.shape[1]
    tm, tn = INPROJ_TM, INPROJ_TN
    n_rope = 2 * diff_w // tn
    q_ranges = ((0, diff_w), (3 * diff_w, 3 * diff_w + fox_w))
    body = functools.partial(_inproj_body, n_rope=n_rope, tn=tn, q_ranges=q_ranges)
    return pl.pallas_call(
        body,
        out_shape=(jax.ShapeDtypeStruct((n, cols), BF16), jax.ShapeDtypeStruct((n, LANES), F32)),
        grid=(n // tm, cols // tn),
        in_specs=[
            pl.BlockSpec((tm, d), lambda i, j: (i, 0)),
            pl.BlockSpec((1, d), lambda i, j: (0, 0)),
            pl.BlockSpec((d, tn), lambda i, j: (0, j)),
            pl.BlockSpec((d, LANES), lambda i, j: (0, 0)),
            pl.BlockSpec((tm, LANES), lambda i, j: (i, 0)),
            pl.BlockSpec((tm, LANES), lambda i, j: (i, 0)),
        ],
        out_specs=(pl.BlockSpec((tm, tn), lambda i, j: (i, j)),
                   pl.BlockSpec((tm, LANES), lambda i, j: (i, 0))),
        scratch_shapes=[pltpu.VMEM((tm, d), BF16)],
        compiler_params=_cparams(("parallel", "arbitrary")),
        name="inproj",
    )(x2, g, w_main, w_f, cos, sin_signed)


def _cumgate_body(z_ref, b_ref, c_ref):
    z = z_ref[...] + b_ref[...]
    v = jnp.minimum(z, 0.0) - jnp.log(1.0 + jnp.exp(-jnp.abs(z)))
    seq = v.shape[1]
    lane = lax.broadcasted_iota(jnp.int32, v.shape, 1)
    shift = 1
    while shift < seq:
        v = v + jnp.where(lane >= shift, pltpu.roll(v, shift, 1), 0.0)
        shift *= 2
    c_ref[...] = v


def _cumgate(z_rows, b_rows):
    return pl.pallas_call(
        _cumgate_body,
        out_shape=jax.ShapeDtypeStruct(z_rows.shape, F32),
        name="cumgate",
    )(z_rows, b_rows)


def _dot_nt(a, b):
    return lax.dot_general(a, b, (((1,), (1,)), ((), ())), preferred_element_type=F32)


def _build_vt(v_ref, vt_ref):
    seq, dv = v_ref.shape
    for c in range(seq // VT_CHUNK):
        blk = v_ref[c * VT_CHUNK:(c + 1) * VT_CHUNK, :].astype(F32)
        vt_ref[0:dv, c * VT_CHUNK:(c + 1) * VT_CHUNK] = blk.T.astype(BF16)
    vt_ref[dv:dv + ONES_ROWS, :] = jnp.ones((ONES_ROWS, seq), BF16)


def _softmax_step(s_t, vt_blk, m_ref, acc_ref):
    m_prev = m_ref[...]
    m_new = jnp.maximum(m_prev, jnp.max(s_t, axis=0, keepdims=True))
    alpha = jnp.exp2(m_prev - m_new)
    p_t = jnp.exp2(s_t - m_new).astype(BF16)
    acc_ref[...] = alpha * acc_ref[...] + jnp.dot(vt_blk, p_t, preferred_element_type=F32)
    m_ref[...] = m_new


def _sweep_keys(qi, t, step):
    def pair(i, carry):
        step(pl.multiple_of(i * (2 * t), 2 * t), 2 * t, False)
        return carry

    lax.fori_loop(0, qi // 2, pair, 0)

    @pl.when(qi % 2 == 1)
    def _():
        step(pl.multiple_of((qi - 1) * t, t), t, False)

    step(pl.multiple_of(qi * t, t), t, True)


def _normalised(acc, dv):
    return acc[0:dv] * (1.0 / acc[dv:dv + 1])


def _diff_body(lq1_ref, lk1_ref, lq2_ref, lk2_ref, gsub_ref, q1_ref, q2_ref, k1_ref, k2_ref, v_ref,
               o_ref, vt, m1, a1, m2, a2, *, t, lam_init):
    qi = pl.program_id(2)
    dv = v_ref.shape[1]

    @pl.when(qi == 0)
    def _():
        _build_vt(v_ref, vt)

    for m_ref, a_ref in ((m1, a1), (m2, a2)):
        m_ref[...] = jnp.full(m_ref.shape, NEG_BIG, F32)
        a_ref[...] = jnp.zeros(a_ref.shape, F32)
    q1 = q1_ref[...]
    q2 = q2_ref[...]
    key_chunk = lax.broadcasted_iota(jnp.int32, (t, t), 0) // CHUNK
    qry_chunk = lax.broadcasted_iota(jnp.int32, (t, t), 1) // CHUNK
    visible = key_chunk <= qry_chunk

    def step(off, size, diagonal):
        vt_blk = vt[:, pl.ds(off, size)]
        for q, k_ref, m_ref, a_ref in ((q1, k1_ref, m1, a1), (q2, k2_ref, m2, a2)):
            s_t = _dot_nt(k_ref[pl.ds(off, size), :], q)
            if diagonal:
                s_t = jnp.where(visible, s_t, NEG_BIG)
            _softmax_step(s_t, vt_blk, m_ref, a_ref)

    _sweep_keys(qi, t, step)

    lam = (jnp.exp(jnp.sum(lq1_ref[...] * lk1_ref[...], axis=-1, keepdims=True))
           - jnp.exp(jnp.sum(lq2_ref[...] * lk2_ref[...], axis=-1, keepdims=True)) + lam_init)
    o = (_normalised(a1[...], dv) - lam * _normalised(a2[...], dv)).T
    ms = jnp.mean(o * o, axis=-1, keepdims=True)
    y = o * lax.rsqrt(ms + NORM_EPS) * gsub_ref[...]
    o_ref[...] = (y * (1.0 - lam_init)).astype(BF16)


def _diff_attention(proj, lq1, lk1, lq2, lk2, gsub, *, bsz, seq, diff_w, lam_init):
    t = ATTN_T
    dv = 2 * HEAD_DIM
    n_heads = diff_w // dv
    nq = seq // t
    kcol = diff_w // HEAD_DIM
    vcol = 2 * diff_w // dv
    vec = pl.BlockSpec((1, HEAD_DIM), lambda b, h, i: (0, 0))
    body = functools.partial(_diff_body, t=t, lam_init=lam_init)
    return pl.pallas_call(
        body,
        out_shape=jax.ShapeDtypeStruct((bsz * seq, diff_w), BF16),
        grid=(bsz, n_heads, nq),
        in_specs=[
            vec, vec, vec, vec,
            pl.BlockSpec((1, dv), lambda b, h, i: (0, 0)),
            pl.BlockSpec((t, HEAD_DIM), lambda b, h, i: (b * nq + i, 2 * h)),
            pl.BlockSpec((t, HEAD_DIM), lambda b, h, i: (b * nq + i, 2 * h + 1)),
            pl.BlockSpec((seq, HEAD_DIM), lambda b, h, i: (b, kcol + 2 * h)),
            pl.BlockSpec((seq, HEAD_DIM), lambda b, h, i: (b, kcol + 2 * h + 1)),
            pl.BlockSpec((seq, dv), lambda b, h, i: (b, vcol + h)),
        ],
        out_specs=pl.BlockSpec((t, dv), lambda b, h, i: (b * nq + i, h)),
        scratch_shapes=[pltpu.VMEM((dv + ONES_ROWS, seq), BF16),
                        pltpu.VMEM((1, t), F32), pltpu.VMEM((dv + ONES_ROWS, t), F32),
                        pltpu.VMEM((1, t), F32), pltpu.VMEM((dv + ONES_ROWS, t), F32)],
        compiler_params=_cparams(("arbitrary", "arbitrary", "arbitrary")),
        name="diffattn",
    )(lq1, lk1, lq2, lk2, gsub, proj, proj, proj, proj, proj)


def _fox_body(g_ref, q_ref, k_ref, ck_ref, v_ref, o_ref, vt, ck_lanes, m, acc, *, t):
    qi = pl.program_id(2)
    dv = v_ref.shape[1]

    @pl.when(qi == 0)
    def _():
        _build_vt(v_ref, vt)
        ck_lanes[...] = jnp.broadcast_to(ck_ref[...], ck_lanes.shape)

    m[...] = jnp.full(m.shape, NEG_BIG, F32)
    acc[...] = jnp.zeros(acc.shape, F32)
    q = q_ref[...]
    causal = lax.broadcasted_iota(jnp.int32, (t, t), 0) <= lax.broadcasted_iota(jnp.int32, (t, t), 1)

    def step(off, size, diagonal):
        ck = ck_lanes[pl.ds(off, size), :]
        s_t = _dot_nt(k_ref[pl.ds(off, size), :], q) - jnp.concatenate([ck] * (t // LANES), axis=1)
        if diagonal:
            s_t = jnp.where(causal, s_t, NEG_BIG)
        _softmax_step(s_t, vt[:, pl.ds(off, size)], m, acc)

    _sweep_keys(qi, t, step)

    o = _normalised(acc[...], dv).T
    ms = jnp.mean(o * o, axis=-1, keepdims=True)
    o_ref[...] = (o * lax.rsqrt(ms + NORM_EPS) * g_ref[...]).astype(BF16)


def _fox_attention(proj, c_cols, g_fox, *, bsz, seq, diff_w, fox_w):
    t = ATTN_T
    n_heads = fox_w // HEAD_DIM
    nq = seq // t
    qcol = 3 * diff_w // HEAD_DIM
    kcol = qcol + n_heads
    vcol = kcol + n_heads
    body = functools.partial(_fox_body, t=t)
    return pl.pallas_call(
        body,
        out_shape=jax.ShapeDtypeStruct((bsz * seq, fox_w), BF16),
        grid=(bsz, n_heads, nq),
        in_specs=[
            pl.BlockSpec((1, HEAD_DIM), lambda b, h, i: (0, 0)),
            pl.BlockSpec((t, HEAD_DIM), lambda b, h, i: (b * nq + i, qcol + h)),
            pl.BlockSpec((seq, HEAD_DIM), lambda b, h, i: (b, kcol + h)),
            pl.BlockSpec((None, seq, 1), lambda b, h, i: (b * n_heads + h, 0, 0)),
            pl.BlockSpec((seq, HEAD_DIM), lambda b, h, i: (b, vcol + h)),
        ],
        out_specs=pl.BlockSpec((t, HEAD_DIM), lambda b, h, i: (b * nq + i, h)),
        scratch_shapes=[pltpu.VMEM((HEAD_DIM + ONES_ROWS, seq), BF16), pltpu.VMEM((seq, LANES), F32),
                        pltpu.VMEM((1, t), F32), pltpu.VMEM((HEAD_DIM + ONES_ROWS, t), F32)],
        compiler_params=_cparams(("arbitrary", "arbitrary", "arbitrary")),
        name="foxattn",
    )(g_fox, proj, proj, c_cols, proj)


def _outproj_body(od_ref, of_ref, wd_ref, wf_ref, x_ref, g_ref, wr_ref, br_ref, h_ref, hn_ref, lg_ref):
    h = (x_ref[...]
         + jnp.dot(od_ref[...], wd_ref[...], preferred_element_type=F32)
         + jnp.dot(of_ref[...], wf_ref[...], preferred_element_type=F32))
    h_ref[...] = h
    ms = jnp.mean(h * h, axis=-1, keepdims=True)
    hn = h * lax.rsqrt(ms + NORM_EPS) * g_ref[...]
    hn_ref[...] = hn
    lg_ref[...] = jnp.dot(hn, wr_ref[...], precision=HIGHEST, preferred_element_type=F32) + br_ref[...]


def _outproj(od, of, w_od, w_of, x2, g_ffn, w_router_p, b_router_p):
    n, d = x2.shape
    tm = OUTPROJ_TM
    row = lambda i: (i, 0)
    const = lambda i: (0, 0)
    return pl.pallas_call(
        _outproj_body,
        out_shape=(jax.ShapeDtypeStruct((n, d), F32), jax.ShapeDtypeStruct((n, d), F32),
                   jax.ShapeDtypeStruct((n, LANES), F32)),
        grid=(n // tm,),
        in_specs=[
            pl.BlockSpec((tm, od.shape[1]), row),
            pl.BlockSpec((tm, of.shape[1]), row),
            pl.BlockSpec(w_od.shape, const),
            pl.BlockSpec(w_of.shape, const),
            pl.BlockSpec((tm, d), row),
            pl.BlockSpec((1, d), const),
            pl.BlockSpec((d, LANES), const),
            pl.BlockSpec((1, LANES), const),
        ],
        out_specs=(pl.BlockSpec((tm, d), row), pl.BlockSpec((tm, d), row), pl.BlockSpec((tm, LANES), row)),
        compiler_params=_cparams(("parallel",)),
        name="outproj",
    )(od, of, w_od, w_of, x2, g_ffn, w_router_p, b_router_p)


def _route_body(lg_ref, e_ref, gate_ref, rank_ref, cnt_ref, carry_ref, *, n_experts):
    i = pl.program_id(0)
    tm = lg_ref.shape[0]

    @pl.when(i == 0)
    def _():
        carry_ref[...] = jnp.zeros(carry_ref.shape, F32)

    lane = lax.broadcasted_iota(jnp.int32, (tm, LANES), 1)
    logits = jnp.where(lane < n_experts, lg_ref[...], -jnp.inf)
    vals, idxs, hots = [], [], []
    for _ in range(TOP_K):
        mx = jnp.max(logits, axis=-1, keepdims=True)
        idx = jnp.min(jnp.where(logits == mx, lane, LANES), axis=-1, keepdims=True)
        hot = lane == idx
        logits = jnp.where(hot, -jnp.inf, logits)
        vals.append(mx)
        idxs.append(idx)
        hots.append(hot)

    exps = [jnp.exp(v - vals[0]) for v in vals]
    denom = exps[0]
    for ex in exps[1:]:
        denom = denom + ex
    inv = 1.0 / denom

    hot_sum = jnp.zeros((tm, LANES), F32)
    for hot in hots:
        hot_sum = hot_sum + jnp.where(hot, 1.0, 0.0)
    r = lax.broadcasted_iota(jnp.int32, (tm, tm), 0)
    c = lax.broadcasted_iota(jnp.int32, (tm, tm), 1)
    earlier = jnp.where(c < r, 1.0, 0.0).astype(BF16)
    base = carry_ref[...] + jnp.dot(earlier, hot_sum.astype(BF16), preferred_element_type=F32)

    e_out = jnp.zeros((tm, LANES), jnp.int32)
    g_out = jnp.zeros((tm, LANES), F32)
    r_out = jnp.zeros((tm, LANES), jnp.int32)
    for k in range(TOP_K):
        rank = jnp.sum(jnp.where(hots[k], base, 0.0), axis=-1, keepdims=True).astype(jnp.int32)
        e_out = jnp.where(lane == k, idxs[k], e_out)
        g_out = jnp.where(lane == k, exps[k] * inv, g_out)
        r_out = jnp.where(lane == k, rank, r_out)
    e_ref[...] = e_out
    gate_ref[...] = g_out
    rank_ref[...] = r_out
    carry_ref[...] = carry_ref[...] + jnp.sum(hot_sum, axis=0, keepdims=True)
    cnt_ref[...] = carry_ref[...]


def _route(logits, n_experts):
    n = logits.shape[0]
    tm = ROUTE_TM
    row = lambda i: (i, 0)
    body = functools.partial(_route_body, n_experts=n_experts)
    return pl.pallas_call(
        body,
        out_shape=(jax.ShapeDtypeStruct((n, LANES), jnp.int32), jax.ShapeDtypeStruct((n, LANES), F32),
                   jax.ShapeDtypeStruct((n, LANES), jnp.int32), jax.ShapeDtypeStruct((1, LANES), F32)),
        grid=(n // tm,),
        in_specs=[pl.BlockSpec((tm, LANES), row)],
        out_specs=(pl.BlockSpec((tm, LANES), row), pl.BlockSpec((tm, LANES), row),
                   pl.BlockSpec((tm, LANES), row), pl.BlockSpec((1, LANES), lambda i: (0, 0))),
        scratch_shapes=[pltpu.VMEM((1, LANES), F32)],
        compiler_params=_cparams(("arbitrary",)),
        name="route",
    )(logits)


def _dispatch_body(dest_ref, hn_ref, xs_in_ref, xs_ref, sem):
    del xs_in_ref
    tm = hn_ref.shape[0]

    def start(t, carry):
        src = hn_ref.at[pl.ds(t, 1), :]
        for k in range(TOP_K):
            pltpu.make_async_copy(src, xs_ref.at[pl.ds(dest_ref[0, t * TOP_K + k], 1), :], sem).start()
        return carry

    lax.fori_loop(0, tm, start, 0)
    for _ in range(TOP_K):
        pltpu.make_async_copy(hn_ref, xs_ref.at[pl.ds(0, tm), :], sem).wait()


def _dispatch(dest, hn, xs_init):
    n, d = hn.shape
    tm = DISPATCH_TM
    dest3 = dest.reshape(n // tm, 1, tm * TOP_K)
    return pl.pallas_call(
        _dispatch_body,
        out_shape=jax.ShapeDtypeStruct(xs_init.shape, xs_init.dtype),
        grid=(n // tm,),
        in_specs=[
            pl.BlockSpec((None, 1, tm * TOP_K), lambda i: (i, 0, 0), memory_space=pltpu.SMEM),
            pl.BlockSpec((tm, d), lambda i: (i, 0)),
            pl.BlockSpec(memory_space=pl.ANY),
        ],
        out_specs=pl.BlockSpec(memory_space=pl.ANY),
        scratch_shapes=[pltpu.SemaphoreType.DMA],
        input_output_aliases={2: 0},
        compiler_params=_cparams(("arbitrary",)),
        name="dispatch",
    )(dest3, hn, xs_init)


def _experts_body(te_ref, nu_ref, x_ref, wg_ref, bg_ref, wu_ref, bu_ref, wd_ref, bd_ref, o_ref, xb_ref):
    i = pl.program_id(0)
    j = pl.program_id(1)

    @pl.when(i < nu_ref[0])
    def _():
        @pl.when(j == 0)
        def _():
            xb_ref[...] = x_ref[...].astype(BF16)
            o_ref[...] = jnp.broadcast_to(bd_ref[...], o_ref.shape)

        xb = xb_ref[...]
        g = jnp.dot(xb, wg_ref[...], preferred_element_type=F32) + bg_ref[...]
        u = jnp.dot(xb, wu_ref[...], preferred_element_type=F32) + bu_ref[...]
        g = jnp.minimum(g, SWIGLU_LIMIT)
        u = jnp.clip(u, -SWIGLU_LIMIT, SWIGLU_LIMIT)
        hdn = g * (1.0 / (1.0 + jnp.exp(-SWIGLU_ALPHA * g))) * (u + 1.0)
        o_ref[...] += jnp.dot(hdn.astype(BF16), wd_ref[...], preferred_element_type=F32)

    @pl.when((i >= nu_ref[0]) & (j == 0))
    def _():
        o_ref[...] = jnp.zeros(o_ref.shape, F32)


def _experts(tile_expert, n_used, xs, wg, bg, wu, bu, wd, bd):
    rows, d = xs.shape
    d_exp = wg.shape[2]
    tm, tc = EXPERT_TM, EXPERT_TC
    n_tiles = rows // tm
    nj = d_exp // tc

    def tile(i, nu):
        return jnp.minimum(i, nu[0] - 1)

    def chunk(i, j, nu):
        return jnp.where(i < nu[0], j, nj - 1)

    return pl.pallas_call(
        _experts_body,
        out_shape=jax.ShapeDtypeStruct((rows, d), F32),
        grid_spec=pltpu.PrefetchScalarGridSpec(
            num_scalar_prefetch=2,
            grid=(n_tiles, nj),
            in_specs=[
                pl.BlockSpec((tm, d), lambda i, j, te, nu: (tile(i, nu), 0)),
                pl.BlockSpec((None, d, tc), lambda i, j, te, nu: (te[i], 0, chunk(i, j, nu))),
                pl.BlockSpec((None, 1, tc), lambda i, j, te, nu: (te[i], 0, chunk(i, j, nu))),
                pl.BlockSpec((None, d, tc), lambda i, j, te, nu: (te[i], 0, chunk(i, j, nu))),
                pl.BlockSpec((None, 1, tc), lambda i, j, te, nu: (te[i], 0, chunk(i, j, nu))),
                pl.BlockSpec((None, tc, d), lambda i, j, te, nu: (te[i], chunk(i, j, nu), 0)),
                pl.BlockSpec((None, 1, d), lambda i, j, te, nu: (te[i], 0, 0)),
            ],
            out_specs=pl.BlockSpec((tm, d), lambda i, j, te, nu: (i, 0)),
            scratch_shapes=[pltpu.VMEM((tm, d), BF16)],
        ),
        compiler_params=_cparams(("arbitrary", "arbitrary")),
        name="experts",
    )(tile_expert, n_used, xs, wg, bg, wu, bu, wd, bd)


def _combine_body(dcur_ref, dnext_ref, gate_ref, h_ref, gfin_ref, ys_ref, o_ref, buf, sems, *, final_norm):
    i = pl.program_id(0)
    nb = pl.num_programs(0)
    tm = h_ref.shape[0]
    slot = i % 2

    def start_all(dref, s):
        def go(t, carry):
            for k in range(TOP_K):
                pltpu.make_async_copy(ys_ref.at[pl.ds(dref[0, t * TOP_K + k], 1), :],
                                      buf.at[s, k, pl.ds(t, 1), :], sems.at[s]).start()
            return carry
        lax.fori_loop(0, tm, go, 0)

    @pl.when(i == 0)
    def _():
        start_all(dcur_ref, 0)

    @pl.when(i + 1 < nb)
    def _():
        start_all(dnext_ref, 1 - slot)

    for k in range(TOP_K):
        pltpu.make_async_copy(ys_ref.at[pl.ds(0, tm), :], buf.at[slot, k], sems.at[slot]).wait()

    y = h_ref[...]
    gates = gate_ref[...]
    for k in range(TOP_K):
        y = y + gates[:, k:k + 1] * buf[slot, k]
    if final_norm:
        ms = jnp.mean(y * y, axis=-1, keepdims=True)
        y = y * lax.rsqrt(ms + NORM_EPS) * gfin_ref[...]
    o_ref[...] = y


def _combine(dest, gates, h, g_final, ys, *, final_norm):
    n, d = h.shape
    tm = COMBINE_TM
    nb = n // tm
    dest3 = dest.reshape(nb, 1, tm * TOP_K)
    smem_blk = (None, 1, tm * TOP_K)
    return pl.pallas_call(
        functools.partial(_combine_body, final_norm=final_norm),
        out_shape=jax.ShapeDtypeStruct((n, d), F32),
        grid=(nb,),
        in_specs=[
            pl.BlockSpec(smem_blk, lambda i: (i, 0, 0), memory_space=pltpu.SMEM),
            pl.BlockSpec(smem_blk, lambda i: (jnp.minimum(i + 1, nb - 1), 0, 0), memory_space=pltpu.SMEM),
            pl.BlockSpec((tm, LANES), lambda i: (i, 0)),
            pl.BlockSpec((tm, d), lambda i: (i, 0)),
            pl.BlockSpec((1, d), lambda i: (0, 0)),
            pl.BlockSpec(memory_space=pl.ANY),
        ],
        out_specs=pl.BlockSpec((tm, d), lambda i: (i, 0)),
        scratch_shapes=[pltpu.VMEM((2, TOP_K, tm, d), F32), pltpu.SemaphoreType.DMA((2,))],
        compiler_params=_cparams(("arbitrary",)),
        name="combine",
    )(dest3, dest3, gates, h, g_final, ys)


def _rope_tables(positions):
    inv_freq = ROPE_THETA ** (-jnp.arange(0, HEAD_DIM, 2, dtype=F32) / HEAD_DIM)
    ang = positions.astype(F32)[..., None] * inv_freq
    ang = jnp.concatenate([ang, ang], axis=-1)
    sign = jnp.concatenate([-jnp.ones((HEAD_DIM // 2,), F32), jnp.ones((HEAD_DIM // 2,), F32)])
    return jnp.cos(ang), jnp.sin(ang) * sign


def _layer(x2, cos, sin_signed, layer_idx, bsz, seq, g_attn_norm, w_in, b_forget, lambda_q1, lambda_k1,
           lambda_q2, lambda_k2, g_diff_sub, g_fox_out, w_out, g_ffn_norm, w_router, b_router,
           w_gate, b_gate, w_up, b_up, w_down, b_down):
    n, d = x2.shape
    diff_w = d // 2
    fox_w = d - diff_w
    fox_heads = fox_w // HEAD_DIM
    main_cols = 3 * diff_w + 3 * fox_w
    n_experts = w_router.shape[1]
    lam_init = _lambda_init(layer_idx)

    w_main = w_in[:, :main_cols].astype(BF16)
    w_f = jnp.pad(w_in[:, main_cols:], ((0, 0), (0, LANES - fox_heads)))
    proj, f_logit = _inproj(x2, g_attn_norm[None, :], w_main, w_f, cos, sin_signed, diff_w=diff_w, fox_w=fox_w)

    z_rows = f_logit[:, :fox_heads].reshape(bsz, seq, fox_heads).transpose(0, 2, 1).reshape(bsz * fox_heads, seq)
    b_rows = jnp.tile(b_forget.astype(F32), bsz)[:, None]
    c_rows = _cumgate(z_rows, b_rows)

    od = _diff_attention(proj, lambda_q1[None, :], lambda_k1[None, :], lambda_q2[None, :], lambda_k2[None, :],
                         g_diff_sub[None, :], bsz=bsz, seq=seq, diff_w=diff_w, lam_init=lam_init)
    of = _fox_attention(proj, (c_rows * LOG2E)[:, :, None], g_fox_out[None, :],
                        bsz=bsz, seq=seq, diff_w=diff_w, fox_w=fox_w)

    w_out_b = w_out.astype(BF16)
    w_router_p = jnp.pad(w_router, ((0, 0), (0, LANES - n_experts)))
    b_router_p = jnp.pad(b_router, (0, LANES - n_experts))[None, :]
    h, hn, logits = _outproj(od, of, w_out_b[:diff_w], w_out_b[diff_w:], x2, g_ffn_norm[None, :],
                             w_router_p, b_router_p)

    e_pad, gate_pad, rank_pad, cnt = _route(logits, n_experts)
    e_idx = e_pad[:, :TOP_K]
    counts = cnt[0, :n_experts].astype(jnp.int32)
    padded = (counts + EXPERT_TM - 1) // EXPERT_TM * EXPERT_TM
    pad_ends = jnp.cumsum(padded)
    pad_starts = pad_ends - padded
    start_of = jnp.sum(jnp.where(e_idx[..., None] == jnp.arange(n_experts), pad_starts, 0), axis=-1)
    dest = (start_of + rank_pad[:, :TOP_K]).astype(jnp.int32)
    n_tiles = -(-n * TOP_K // EXPERT_TM) + n_experts
    tile_start = jnp.arange(n_tiles, dtype=jnp.int32) * EXPERT_TM
    tile_expert = jnp.sum((pad_ends[None, :] <= tile_start[:, None]).astype(jnp.int32), axis=1)
    tile_expert = jnp.minimum(tile_expert, n_experts - 1)
    n_used = (pad_ends[-1:] // EXPERT_TM).astype(jnp.int32)
    tile_expert = jnp.where(jnp.arange(n_tiles) < n_used[0], tile_expert, tile_expert[n_used[0] - 1])

    xs = _dispatch(dest, hn, jnp.zeros((n_tiles * EXPERT_TM, d), F32))
    ys = _experts(tile_expert, n_used, xs,
                  w_gate.astype(BF16), b_gate[:, None, :], w_up.astype(BF16), b_up[:, None, :],
                  w_down.astype(BF16), b_down[:, None, :])
    return dest, gate_pad, h, ys


def kernel(x, positions, g_attn_norm, w_in, b_forget, lambda_q1, lambda_k1, lambda_q2, lambda_k2, g_diff_sub,
           g_fox_out, w_out, g_ffn_norm, w_router, b_router, w_gate, b_gate, w_up, b_up, w_down, b_down,
           g_final):
    bsz, seq, d = x.shape
    depth = w_in.shape[0]
    cos, sin_signed = _rope_tables(positions)
    cos = cos.reshape(bsz * seq, HEAD_DIM)
    sin_signed = sin_signed.reshape(bsz * seq, HEAD_DIM)
    x2 = x.reshape(bsz * seq, d)
    for l in range(depth):
        dest, gates, h, ys = _layer(
            x2, cos, sin_signed, l, bsz, seq, g_attn_norm[l], w_in[l], b_forget[l], lambda_q1[l], lambda_k1[l],
            lambda_q2[l], lambda_k2[l], g_diff_sub[l], g_fox_out[l], w_out[l], g_ffn_norm[l], w_router[l],
            b_router[l], w_gate[l], b_gate[l], w_up[l], b_up[l], w_down[l], b_down[l])
        x2 = _combine(dest, gates, h, g_final[None, :], ys, final_norm=(l == depth - 1))
    return x2.reshape(bsz, seq, d)
```

```python
import functools
import math

import jax
import jax.numpy as jnp
from jax import lax
from jax.experimental import pallas as pl
from jax.experimental.pallas import tpu as pltpu

F32 = jnp.float32
BF16 = jnp.bfloat16

HEAD_DIM = 128
CHUNK = 64
ROPE_THETA = 10000.0
NORM_EPS = 1e-5
NEG_BIG = -1e30
TOP_K = 4
SWIGLU_LIMIT = 7.0
SWIGLU_ALPHA = 1.702
LANES = 128
QK_SCALE = HEAD_DIM ** -0.5
LOG2E = math.log2(math.e)
ONES_ROWS = 16

INPROJ_TM = 512
INPROJ_TN = 512
ATTN_T = 512
VT_CHUNK = 512
OUTPROJ_TM = 512
ROUTE_TM = 512
DISPATCH_TM = 256
EXPERT_TM = 512
EXPERT_TC = 512
COMBINE_TM = 128
VMEM_LIMIT = 48 * 1024 * 1024


def _lambda_init(layer_idx):
    return 0.8 - 0.6 * math.exp(-0.3 * layer_idx)


def _cparams(sem):
    return pltpu.CompilerParams(dimension_semantics=sem, vmem_limit_bytes=VMEM_LIMIT)


def _split_bf16(x):
    hi = x.astype(BF16)
    return hi, (x - hi.astype(F32)).astype(BF16)


def _dot3(x_hi, x_lo, w_hi, w_lo):
    dot = functools.partial(jnp.dot, preferred_element_type=F32)
    return dot(x_hi, w_hi) + (dot(x_hi, w_lo) + dot(x_lo, w_hi))


def _inproj_body(x_ref, g_ref, w_ref, wfh_ref, wfl_ref, cos_ref, sin_ref, o_ref, fl_ref, xn_ref,
                 *, n_rope, tn, q_ranges):
    j = pl.program_id(1)

    @pl.when(j == 0)
    def _():
        x = x_ref[...]
        ms = jnp.mean(x * x, axis=-1, keepdims=True)
        xn = x * lax.rsqrt(ms + NORM_EPS) * g_ref[...]
        xn_hi, xn_lo = _split_bf16(xn)
        xn_ref[...] = xn_hi
        fl_ref[...] = _dot3(xn_hi, xn_lo, wfh_ref[...], wfl_ref[...])

    acc = jnp.dot(xn_ref[...], w_ref[...], preferred_element_type=F32)
    col0 = j * tn
    is_q = (col0 >= q_ranges[0][0]) & (col0 < q_ranges[0][1])
    is_q = is_q | ((col0 >= q_ranges[1][0]) & (col0 < q_ranges[1][1]))
    scale = jnp.where(is_q, QK_SCALE * LOG2E, 1.0).astype(F32)

    @pl.when(j < n_rope)
    def _():
        cos = cos_ref[...] * scale
        sin = sin_ref[...] * scale
        for gi in range(tn // LANES):
            xg = acc[:, gi * LANES:(gi + 1) * LANES]
            rot = pltpu.roll(xg, HEAD_DIM // 2, 1)
            o_ref[:, gi * LANES:(gi + 1) * LANES] = (xg * cos + rot * sin).astype(BF16)

    @pl.when(j >= n_rope)
    def _():
        o_ref[...] = (acc * scale).astype(BF16)


def _inproj(x2, g, w_main, w_f_hi, w_f_lo, cos, sin_signed, *, diff_w, fox_w):
    n, d = x2.shape
    cols = w_main.shape[1]
    tm, tn = INPROJ_TM, INPROJ_TN
    n_rope = 2 * diff_w // tn
    q_ranges = ((0, diff_w), (3 * diff_w, 3 * diff_w + fox_w))
    body = functools.partial(_inproj_body, n_rope=n_rope, tn=tn, q_ranges=q_ranges)
    return pl.pallas_call(
        body,
        out_shape=(jax.ShapeDtypeStruct((n, cols), BF16), jax.ShapeDtypeStruct((n, LANES), F32)),
        grid=(n // tm, cols // tn),
        in_specs=[
            pl.BlockSpec((tm, d), lambda i, j: (i, 0)),
            pl.BlockSpec((1, d), lambda i, j: (0, 0)),
            pl.BlockSpec((d, tn), lambda i, j: (0, j)),
            pl.BlockSpec((d, LANES), lambda i, j: (0, 0)),
            pl.BlockSpec((d, LANES), lambda i, j: (0, 0)),
            pl.BlockSpec((tm, LANES), lambda i, j: (i, 0)),
            pl.BlockSpec((tm, LANES), lambda i, j: (i, 0)),
        ],
        out_specs=(pl.BlockSpec((tm, tn), lambda i, j: (i, j)),
                   pl.BlockSpec((tm, LANES), lambda i, j: (i, 0))),
        scratch_shapes=[pltpu.VMEM((tm, d), BF16)],
        compiler_params=_cparams(("parallel", "arbitrary")),
        name="inproj",
    )(x2, g, w_main, w_f_hi, w_f_lo, cos, sin_signed)


def _cumgate_body(z_ref, b_ref, c_ref):
    z = z_ref[...] + b_ref[...]
    v = jnp.minimum(z, 0.0) - jnp.log(1.0 + jnp.exp(-jnp.abs(z)))
    seq = v.shape[1]
    lane = lax.broadcasted_iota(jnp.int32, v.shape, 1)
    shift = 1
    while shift < seq:
        v = v + jnp.where(lane >= shift, pltpu.roll(v, shift, 1), 0.0)
        shift *= 2
    c_ref[...] = v


def _cumgate(z_rows, b_rows):
    return pl.pallas_call(
        _cumgate_body,
        out_shape=jax.ShapeDtypeStruct(z_rows.shape, F32),
        name="cumgate",
    )(z_rows, b_rows)


def _dot_nt(a, b):
    return lax.dot_general(a, b, (((1,), (1,)), ((), ())), preferred_element_type=F32)


def _build_vt(v_ref, vt_ref):
    seq, dv = v_ref.shape
    for c in range(seq // VT_CHUNK):
        blk = v_ref[c * VT_CHUNK:(c + 1) * VT_CHUNK, :].astype(F32)
        vt_ref[0:dv, c * VT_CHUNK:(c + 1) * VT_CHUNK] = blk.T.astype(BF16)
    vt_ref[dv:dv + ONES_ROWS, :] = jnp.ones((ONES_ROWS, seq), BF16)


def _softmax_step(s_t, vt_blk, m_ref, acc_ref):
    m_prev = m_ref[...]
    m_new = jnp.maximum(m_prev, jnp.max(s_t, axis=0, keepdims=True))
    alpha = jnp.exp2(m_prev - m_new)
    p_t = jnp.exp2(s_t - m_new).astype(BF16)
    acc_ref[...] = alpha * acc_ref[...] + jnp.dot(vt_blk, p_t, preferred_element_type=F32)
    m_ref[...] = m_new


def _sweep_keys(qi, t, step):
    def pair(i, carry):
        step(pl.multiple_of(i * (2 * t), 2 * t), 2 * t, False)
        return carry

    lax.fori_loop(0, qi // 2, pair, 0)

    @pl.when(qi % 2 == 1)
    def _():
        step(pl.multiple_of((qi - 1) * t, t), t, False)

    step(pl.multiple_of(qi * t, t), t, True)


def _normalised(acc, dv):
    return acc[0:dv] * (1.0 / acc[dv:dv + 1])


def _cast_rows_specs(weights, steps, step_of):
    in_specs, out_specs, out_shapes = [], [], []
    for w in weights:
        rows, cols = w.shape
        spec = pl.BlockSpec((rows // steps, cols), lambda b, h, i: (step_of(b, h, i), 0))
        in_specs.append(spec)
        out_specs.append(spec)
        out_shapes.append(jax.ShapeDtypeStruct((rows, cols), BF16))
    return in_specs, out_specs, out_shapes


def _can_cast_rows(weights, steps):
    return all(w.shape[0] % steps == 0 and (w.shape[0] // steps) % 16 == 0 for w in weights)


def _cast_rows(in_refs, out_refs):
    for src, dst in zip(in_refs, out_refs):
        dst[...] = src[...].astype(BF16)


def _diff_body(lq1_ref, lk1_ref, lq2_ref, lk2_ref, gsub_ref, q1_ref, q2_ref, k1_ref, k2_ref, v_ref,
               *rest, t, lam_init, n_cast):
    cast_in, (o_ref, *cast_out), (vt, m1, a1, m2, a2) = rest[:n_cast], rest[n_cast:2 * n_cast + 1], rest[2 * n_cast + 1:]
    _cast_rows(cast_in, cast_out)
    qi = pl.program_id(2)
    dv = v_ref.shape[1]

    @pl.when(qi == 0)
    def _():
        _build_vt(v_ref, vt)

    for m_ref, a_ref in ((m1, a1), (m2, a2)):
        m_ref[...] = jnp.full(m_ref.shape, NEG_BIG, F32)
        a_ref[...] = jnp.zeros(a_ref.shape, F32)
    q1 = q1_ref[...]
    q2 = q2_ref[...]
    key_chunk = lax.broadcasted_iota(jnp.int32, (t, t), 0) // CHUNK
    qry_chunk = lax.broadcasted_iota(jnp.int32, (t, t), 1) // CHUNK
    visible = key_chunk <= qry_chunk

    def step(off, size, diagonal):
        vt_blk = vt[:, pl.ds(off, size)]
        for q, k_ref, m_ref, a_ref in ((q1, k1_ref, m1, a1), (q2, k2_ref, m2, a2)):
            s_t = _dot_nt(k_ref[pl.ds(off, size), :], q)
            if diagonal:
                s_t = jnp.where(visible, s_t, NEG_BIG)
            _softmax_step(s_t, vt_blk, m_ref, a_ref)

    _sweep_keys(qi, t, step)

    lam = (jnp.exp(jnp.sum(lq1_ref[...] * lk1_ref[...], axis=-1, keepdims=True))
           - jnp.exp(jnp.sum(lq2_ref[...] * lk2_ref[...], axis=-1, keepdims=True)) + lam_init)
    o = (_normalised(a1[...], dv) - lam * _normalised(a2[...], dv)).T
    ms = jnp.mean(o * o, axis=-1, keepdims=True)
    y = o * lax.rsqrt(ms + NORM_EPS) * gsub_ref[...]
    o_ref[...] = (y * (1.0 - lam_init)).astype(BF16)


def _diff_attention(proj, lq1, lk1, lq2, lk2, gsub, cast_weights, *, bsz, seq, diff_w, lam_init):
    t = ATTN_T
    dv = 2 * HEAD_DIM
    n_heads = diff_w // dv
    nq = seq // t
    kcol = diff_w // HEAD_DIM
    vcol = 2 * diff_w // dv
    vec = pl.BlockSpec((1, HEAD_DIM), lambda b, h, i: (0, 0))
    cast_in, cast_out, cast_shapes = _cast_rows_specs(
        cast_weights, bsz * n_heads * nq, lambda b, h, i: (b * n_heads + h) * nq + i)
    body = functools.partial(_diff_body, t=t, lam_init=lam_init, n_cast=len(cast_weights))
    return pl.pallas_call(
        body,
        out_shape=(jax.ShapeDtypeStruct((bsz * seq, diff_w), BF16), *cast_shapes),
        grid=(bsz, n_heads, nq),
        in_specs=[
            vec, vec, vec, vec,
            pl.BlockSpec((1, dv), lambda b, h, i: (0, 0)),
            pl.BlockSpec((t, HEAD_DIM), lambda b, h, i: (b * nq + i, 2 * h)),
            pl.BlockSpec((t, HEAD_DIM), lambda b, h, i: (b * nq + i, 2 * h + 1)),
            pl.BlockSpec((seq, HEAD_DIM), lambda b, h, i: (b, kcol + 2 * h)),
            pl.BlockSpec((seq, HEAD_DIM), lambda b, h, i: (b, kcol + 2 * h + 1)),
            pl.BlockSpec((seq, dv), lambda b, h, i: (b, vcol + h)),
            *cast_in,
        ],
        out_specs=(pl.BlockSpec((t, dv), lambda b, h, i: (b * nq + i, h)), *cast_out),
        scratch_shapes=[pltpu.VMEM((dv + ONES_ROWS, seq), BF16),
                        pltpu.VMEM((1, t), F32), pltpu.VMEM((dv + ONES_ROWS, t), F32),
                        pltpu.VMEM((1, t), F32), pltpu.VMEM((dv + ONES_ROWS, t), F32)],
        compiler_params=_cparams(("arbitrary", "arbitrary", "arbitrary")),
        name="diffattn",
    )(lq1, lk1, lq2, lk2, gsub, proj, proj, proj, proj, proj, *cast_weights)


def _fox_body(g_ref, q_ref, k_ref, ck_ref, v_ref, *rest, t, n_cast):
    cast_in, (o_ref, *cast_out), (vt, ck_lanes, m, acc) = rest[:n_cast], rest[n_cast:2 * n_cast + 1], rest[2 * n_cast + 1:]
    _cast_rows(cast_in, cast_out)
    qi = pl.program_id(2)
    dv = v_ref.shape[1]

    @pl.when(qi == 0)
    def _():
        _build_vt(v_ref, vt)
        ck_lanes[...] = jnp.broadcast_to(ck_ref[...], ck_lanes.shape)

    m[...] = jnp.full(m.shape, NEG_BIG, F32)
    acc[...] = jnp.zeros(acc.shape, F32)
    q = q_ref[...]
    causal = lax.broadcasted_iota(jnp.int32, (t, t), 0) <= lax.broadcasted_iota(jnp.int32, (t, t), 1)

    def step(off, size, diagonal):
        ck = ck_lanes[pl.ds(off, size), :]
        s_t = _dot_nt(k_ref[pl.ds(off, size), :], q) - jnp.concatenate([ck] * (t // LANES), axis=1)
        if diagonal:
            s_t = jnp.where(causal, s_t, NEG_BIG)
        _softmax_step(s_t, vt[:, pl.ds(off, size)], m, acc)

    _sweep_keys(qi, t, step)

    o = _normalised(acc[...], dv).T
    ms = jnp.mean(o * o, axis=-1, keepdims=True)
    o_ref[...] = (o * lax.rsqrt(ms + NORM_EPS) * g_ref[...]).astype(BF16)


def _fox_attention(proj, c_cols, g_fox, cast_weights, *, bsz, seq, diff_w, fox_w):
    t = ATTN_T
    n_heads = fox_w // HEAD_DIM
    nq = seq // t
    qcol = 3 * diff_w // HEAD_DIM
    kcol = qcol + n_heads
    vcol = kcol + n_heads
    cast_in, cast_out, cast_shapes = _cast_rows_specs(
        cast_weights, bsz * n_heads * nq, lambda b, h, i: (b * n_heads + h) * nq + i)
    body = functools.partial(_fox_body, t=t, n_cast=len(cast_weights))
    return pl.pallas_call(
        body,
        out_shape=(jax.ShapeDtypeStruct((bsz * seq, fox_w), BF16), *cast_shapes),
        grid=(bsz, n_heads, nq),
        in_specs=[
            pl.BlockSpec((1, HEAD_DIM), lambda b, h, i: (0, 0)),
            pl.BlockSpec((t, HEAD_DIM), lambda b, h, i: (b * nq + i, qcol + h)),
            pl.BlockSpec((seq, HEAD_DIM), lambda b, h, i: (b, kcol + h)),
            pl.BlockSpec((None, seq, 1), lambda b, h, i: (b * n_heads + h, 0, 0)),
            pl.BlockSpec((seq, HEAD_DIM), lambda b, h, i: (b, vcol + h)),
            *cast_in,
        ],
        out_specs=(pl.BlockSpec((t, HEAD_DIM), lambda b, h, i: (b * nq + i, h)), *cast_out),
        scratch_shapes=[pltpu.VMEM((HEAD_DIM + ONES_ROWS, seq), BF16), pltpu.VMEM((seq, LANES), F32),
                        pltpu.VMEM((1, t), F32), pltpu.VMEM((HEAD_DIM + ONES_ROWS, t), F32)],
        compiler_params=_cparams(("arbitrary", "arbitrary", "arbitrary")),
        name="foxattn",
    )(g_fox, proj, proj, c_cols, proj, *cast_weights)


def _outproj_body(od_ref, of_ref, wd_ref, wf_ref, x_ref, g_ref, wrh_ref, wrl_ref, br_ref, h_ref, hn_ref, lg_ref):
    h = (x_ref[...]
         + jnp.dot(od_ref[...], wd_ref[...], preferred_element_type=F32)
         + jnp.dot(of_ref[...], wf_ref[...], preferred_element_type=F32))
    h_ref[...] = h
    ms = jnp.mean(h * h, axis=-1, keepdims=True)
    hn = h * lax.rsqrt(ms + NORM_EPS) * g_ref[...]
    hn_ref[...] = hn
    hn_hi, hn_lo = _split_bf16(hn)
    lg_ref[...] = _dot3(hn_hi, hn_lo, wrh_ref[...], wrl_ref[...]) + br_ref[...]


def _outproj(od, of, w_od, w_of, x2, g_ffn, w_router_hi, w_router_lo, b_router_p):
    n, d = x2.shape
    tm = OUTPROJ_TM
    row = lambda i: (i, 0)
    const = lambda i: (0, 0)
    resident = functools.partial(pl.BlockSpec, index_map=const, pipeline_mode=pl.Buffered(1))
    return pl.pallas_call(
        _outproj_body,
        out_shape=(jax.ShapeDtypeStruct((n, d), F32), jax.ShapeDtypeStruct((n, d), F32),
                   jax.ShapeDtypeStruct((n, LANES), F32)),
        grid=(n // tm,),
        in_specs=[
            pl.BlockSpec((tm, od.shape[1]), row),
            pl.BlockSpec((tm, of.shape[1]), row),
            resident(w_od.shape),
            resident(w_of.shape),
            pl.BlockSpec((tm, d), row),
            pl.BlockSpec((1, d), const),
            resident((d, LANES)),
            resident((d, LANES)),
            pl.BlockSpec((1, LANES), const),
        ],
        out_specs=(pl.BlockSpec((tm, d), row), pl.BlockSpec((tm, d), row), pl.BlockSpec((tm, LANES), row)),
        compiler_params=_cparams(("parallel",)),
        name="outproj",
    )(od, of, w_od, w_of, x2, g_ffn, w_router_hi, w_router_lo, b_router_p)


def _route_body(lg_ref, e_ref, gate_ref, rank_ref, cnt_ref, carry_ref, *, n_experts):
    i = pl.program_id(0)
    tm = lg_ref.shape[0]

    @pl.when(i == 0)
    def _():
        carry_ref[...] = jnp.zeros(carry_ref.shape, F32)

    lane = lax.broadcasted_iota(jnp.int32, (tm, LANES), 1)
    logits = jnp.where(lane < n_experts, lg_ref[...], -jnp.inf)
    vals, idxs, hots = [], [], []
    for _ in range(TOP_K):
        mx = jnp.max(logits, axis=-1, keepdims=True)
        idx = jnp.min(jnp.where(logits == mx, lane, LANES), axis=-1, keepdims=True)
        hot = lane == idx
        logits = jnp.where(hot, -jnp.inf, logits)
        vals.append(mx)
        idxs.append(idx)
        hots.append(hot)

    exps = [jnp.exp(v - vals[0]) for v in vals]
    denom = exps[0]
    for ex in exps[1:]:
        denom = denom + ex
    inv = 1.0 / denom

    hot_sum = jnp.zeros((tm, LANES), F32)
    for hot in hots:
        hot_sum = hot_sum + jnp.where(hot, 1.0, 0.0)
    r = lax.broadcasted_iota(jnp.int32, (tm, tm), 0)
    c = lax.broadcasted_iota(jnp.int32, (tm, tm), 1)
    earlier = jnp.where(c < r, 1.0, 0.0).astype(BF16)
    base = carry_ref[...] + jnp.dot(earlier, hot_sum.astype(BF16), preferred_element_type=F32)

    e_out = jnp.zeros((tm, LANES), jnp.int32)
    g_out = jnp.zeros((tm, LANES), F32)
    r_out = jnp.zeros((tm, LANES), jnp.int32)
    for k in range(TOP_K):
        rank = jnp.sum(jnp.where(hots[k], base, 0.0), axis=-1, keepdims=True).astype(jnp.int32)
        e_out = jnp.where(lane == k, idxs[k], e_out)
        g_out = jnp.where(lane == k, exps[k] * inv, g_out)
        r_out = jnp.where(lane == k, rank, r_out)
    e_ref[...] = e_out
    gate_ref[...] = g_out
    rank_ref[...] = r_out
    carry_ref[...] = carry_ref[...] + jnp.sum(hot_sum, axis=0, keepdims=True)
    cnt_ref[...] = carry_ref[...]


def _route(logits, n_experts):
    n = logits.shape[0]
    tm = ROUTE_TM
    row = lambda i: (i, 0)
    body = functools.partial(_route_body, n_experts=n_experts)
    return pl.pallas_call(
        body,
        out_shape=(jax.ShapeDtypeStruct((n, LANES), jnp.int32), jax.ShapeDtypeStruct((n, LANES), F32),
                   jax.ShapeDtypeStruct((n, LANES), jnp.int32), jax.ShapeDtypeStruct((1, LANES), F32)),
        grid=(n // tm,),
        in_specs=[pl.BlockSpec((tm, LANES), row)],
        out_specs=(pl.BlockSpec((tm, LANES), row), pl.BlockSpec((tm, LANES), row),
                   pl.BlockSpec((tm, LANES), row), pl.BlockSpec((1, LANES), lambda i: (0, 0))),
        scratch_shapes=[pltpu.VMEM((1, LANES), F32)],
        compiler_params=_cparams(("arbitrary",)),
        name="route",
    )(logits)


def _dispatch_body(ztile_ref, dest_ref, hn_ref, xs_ref, zeros, sem, zsem):
    tm = hn_ref.shape[0]

    @pl.when(pl.program_id(0) == 0)
    def _():
        zeros[...] = jnp.zeros(zeros.shape, zeros.dtype)

        def fill(z):
            row0 = pl.multiple_of(ztile_ref[z] * EXPERT_TM, EXPERT_TM)
            return pltpu.make_async_copy(zeros, xs_ref.at[pl.ds(row0, EXPERT_TM), :], zsem)

        def fill_start(z, carry):
            @pl.when(ztile_ref[z] >= 0)
            def _():
                fill(z).start()
            return carry

        def fill_wait(z, carry):
            @pl.when(ztile_ref[z] >= 0)
            def _():
                fill(z).wait()
            return carry

        lax.fori_loop(0, ztile_ref.shape[0], fill_start, 0)
        lax.fori_loop(0, ztile_ref.shape[0], fill_wait, 0)

    def start(t, carry):
        src = hn_ref.at[pl.ds(t, 1), :]
        for k in range(TOP_K):
            pltpu.make_async_copy(src, xs_ref.at[pl.ds(dest_ref[0, t * TOP_K + k], 1), :], sem).start()
        return carry

    lax.fori_loop(0, tm, start, 0)
    for _ in range(TOP_K):
        pltpu.make_async_copy(hn_ref, xs_ref.at[pl.ds(0, tm), :], sem).wait()


def _dispatch(zero_tiles, dest, hn, n_rows):
    n, d = hn.shape
    tm = DISPATCH_TM
    dest3 = dest.reshape(n // tm, 1, tm * TOP_K)
    return pl.pallas_call(
        _dispatch_body,
        out_shape=jax.ShapeDtypeStruct((n_rows, d), hn.dtype),
        grid_spec=pltpu.PrefetchScalarGridSpec(
            num_scalar_prefetch=1,
            grid=(n // tm,),
            in_specs=[
                pl.BlockSpec((None, 1, tm * TOP_K), lambda i, zt: (i, 0, 0), memory_space=pltpu.SMEM),
                pl.BlockSpec((tm, d), lambda i, zt: (i, 0)),
            ],
            out_specs=pl.BlockSpec(memory_space=pl.ANY),
            scratch_shapes=[pltpu.VMEM((EXPERT_TM, d), hn.dtype), pltpu.SemaphoreType.DMA,
                            pltpu.SemaphoreType.DMA],
        ),
        compiler_params=_cparams(("arbitrary",)),
        name="dispatch",
    )(zero_tiles, dest3, hn)


def _experts_body(te_ref, nu_ref, x_ref, wg_ref, bg_ref, wu_ref, bu_ref, wd_ref, bd_ref, o_ref, xb_ref):
    i = pl.program_id(0)
    j = pl.program_id(1)

    @pl.when(i < nu_ref[0])
    def _():
        @pl.when(j == 0)
        def _():
            xb_ref[...] = x_ref[...].astype(BF16)
            o_ref[...] = jnp.broadcast_to(bd_ref[...], o_ref.shape)

        xb = xb_ref[...]
        g = jnp.dot(xb, wg_ref[...], preferred_element_type=F32) + bg_ref[...]
        u = jnp.dot(xb, wu_ref[...], preferred_element_type=F32) + bu_ref[...]
        g = jnp.minimum(g, SWIGLU_LIMIT)
        u = jnp.clip(u, -SWIGLU_LIMIT, SWIGLU_LIMIT)
        hdn = g * (1.0 / (1.0 + jnp.exp(-SWIGLU_ALPHA * g))) * (u + 1.0)
        o_ref[...] += jnp.dot(hdn.astype(BF16), wd_ref[...], preferred_element_type=F32)

    @pl.when((i >= nu_ref[0]) & (j == 0))
    def _():
        o_ref[...] = jnp.zeros(o_ref.shape, F32)


def _experts(tile_expert, n_used, xs, wg, bg, wu, bu, wd, bd):
    rows, d = xs.shape
    d_exp = wg.shape[2]
    tm, tc = EXPERT_TM, EXPERT_TC
    n_tiles = rows // tm
    nj = d_exp // tc

    def tile(i, nu):
        return jnp.minimum(i, nu[0] - 1)

    def chunk(i, j, nu):
        return jnp.where(i < nu[0], j, nj - 1)

    return pl.pallas_call(
        _experts_body,
        out_shape=jax.ShapeDtypeStruct((rows, d), F32),
        grid_spec=pltpu.PrefetchScalarGridSpec(
            num_scalar_prefetch=2,
            grid=(n_tiles, nj),
            in_specs=[
                pl.BlockSpec((tm, d), lambda i, j, te, nu: (tile(i, nu), 0)),
                pl.BlockSpec((None, d, tc), lambda i, j, te, nu: (te[i], 0, chunk(i, j, nu))),
                pl.BlockSpec((None, 1, tc), lambda i, j, te, nu: (te[i], 0, chunk(i, j, nu))),
                pl.BlockSpec((None, d, tc), lambda i, j, te, nu: (te[i], 0, chunk(i, j, nu))),
                pl.BlockSpec((None, 1, tc), lambda i, j, te, nu: (te[i], 0, chunk(i, j, nu))),
                pl.BlockSpec((None, tc, d), lambda i, j, te, nu: (te[i], chunk(i, j, nu), 0)),
                pl.BlockSpec((None, 1, d), lambda i, j, te, nu: (te[i], 0, 0)),
            ],
            out_specs=pl.BlockSpec((tm, d), lambda i, j, te, nu: (i, 0)),
            scratch_shapes=[pltpu.VMEM((tm, d), BF16)],
        ),
        compiler_params=_cparams(("arbitrary", "arbitrary")),
        name="experts",
    )(tile_expert, n_used, xs, wg, bg, wu, bu, wd, bd)


def _combine_body(dcur_ref, dnext_ref, gate_ref, h_ref, gfin_ref, ys_ref, o_ref, buf, sems, *, final_norm):
    i = pl.program_id(0)
    nb = pl.num_programs(0)
    tm = h_ref.shape[0]
    slot = i % 2

    def start_all(dref, s):
        def go(t, carry):
            for k in range(TOP_K):
                pltpu.make_async_copy(ys_ref.at[pl.ds(dref[0, t * TOP_K + k], 1), :],
                                      buf.at[s, k, pl.ds(t, 1), :], sems.at[s]).start()
            return carry
        lax.fori_loop(0, tm, go, 0)

    @pl.when(i == 0)
    def _():
        start_all(dcur_ref, 0)

    @pl.when(i + 1 < nb)
    def _():
        start_all(dnext_ref, 1 - slot)

    for k in range(TOP_K):
        pltpu.make_async_copy(ys_ref.at[pl.ds(0, tm), :], buf.at[slot, k], sems.at[slot]).wait()

    y = h_ref[...]
    gates = gate_ref[...]
    for k in range(TOP_K):
        y = y + gates[:, k:k + 1] * buf[slot, k]
    if final_norm:
        ms = jnp.mean(y * y, axis=-1, keepdims=True)
        y = y * lax.rsqrt(ms + NORM_EPS) * gfin_ref[...]
    o_ref[...] = y


def _combine(dest, gates, h, g_final, ys, *, final_norm):
    n, d = h.shape
    tm = COMBINE_TM
    nb = n // tm
    dest3 = dest.reshape(nb, 1, tm * TOP_K)
    smem_blk = (None, 1, tm * TOP_K)
    return pl.pallas_call(
        functools.partial(_combine_body, final_norm=final_norm),
        out_shape=jax.ShapeDtypeStruct((n, d), F32),
        grid=(nb,),
        in_specs=[
            pl.BlockSpec(smem_blk, lambda i: (i, 0, 0), memory_space=pltpu.SMEM),
            pl.BlockSpec(smem_blk, lambda i: (jnp.minimum(i + 1, nb - 1), 0, 0), memory_space=pltpu.SMEM),
            pl.BlockSpec((tm, LANES), lambda i: (i, 0)),
            pl.BlockSpec((tm, d), lambda i: (i, 0)),
            pl.BlockSpec((1, d), lambda i: (0, 0)),
            pl.BlockSpec(memory_space=pl.ANY),
        ],
        out_specs=pl.BlockSpec((tm, d), lambda i: (i, 0)),
        scratch_shapes=[pltpu.VMEM((2, TOP_K, tm, d), F32), pltpu.SemaphoreType.DMA((2,))],
        compiler_params=_cparams(("arbitrary",)),
        name="combine",
    )(dest3, dest3, gates, h, g_final, ys)


def _rope_tables(positions):
    inv_freq = ROPE_THETA ** (-jnp.arange(0, HEAD_DIM, 2, dtype=F32) / HEAD_DIM)
    ang = positions.astype(F32)[..., None] * inv_freq
    ang = jnp.concatenate([ang, ang], axis=-1)
    sign = jnp.concatenate([-jnp.ones((HEAD_DIM // 2,), F32), jnp.ones((HEAD_DIM // 2,), F32)])
    return jnp.cos(ang), jnp.sin(ang) * sign


def _layer(x2, cos, sin_signed, layer_idx, bsz, seq, g_attn_norm, w_in, b_forget, lambda_q1, lambda_k1,
           lambda_q2, lambda_k2, g_diff_sub, g_fox_out, w_out, g_ffn_norm, w_router, b_router,
           w_gate, b_gate, w_up, b_up, w_down, b_down):
    n, d = x2.shape
    diff_w = d // 2
    fox_w = d - diff_w
    fox_heads = fox_w // HEAD_DIM
    main_cols = 3 * diff_w + 3 * fox_w
    n_experts = w_router.shape[1]
    lam_init = _lambda_init(layer_idx)

    w_main = w_in[:, :main_cols].astype(BF16)
    w_f_hi, w_f_lo = _split_bf16(jnp.pad(w_in[:, main_cols:], ((0, 0), (0, LANES - fox_heads))))
    proj, f_logit = _inproj(x2, g_attn_norm[None, :], w_main, w_f_hi, w_f_lo, cos, sin_signed,
                            diff_w=diff_w, fox_w=fox_w)

    z_rows = f_logit[:, :fox_heads].reshape(bsz, seq, fox_heads).transpose(0, 2, 1).reshape(bsz * fox_heads, seq)
    b_rows = jnp.tile(b_forget.astype(F32), bsz)[:, None]
    c_rows = _cumgate(z_rows, b_rows)

    d_exp = w_gate.shape[2]
    gate_up = (w_gate.reshape(n_experts * d, d_exp), w_up.reshape(n_experts * d, d_exp))
    down = (w_down.reshape(n_experts * d_exp, d),)
    diff_steps = bsz * (diff_w // (2 * HEAD_DIM)) * (seq // ATTN_T)
    fox_steps = bsz * fox_heads * (seq // ATTN_T)
    if not _can_cast_rows(down, diff_steps):
        down_b, down = tuple(w.astype(BF16) for w in down), ()
    if not _can_cast_rows(gate_up, fox_steps):
        gate_up_b, gate_up = tuple(w.astype(BF16) for w in gate_up), ()

    diff_out = _diff_attention(proj, lambda_q1[None, :], lambda_k1[None, :], lambda_q2[None, :],
                               lambda_k2[None, :], g_diff_sub[None, :], down,
                               bsz=bsz, seq=seq, diff_w=diff_w, lam_init=lam_init)
    fox_out = _fox_attention(proj, (c_rows * LOG2E)[:, :, None], g_fox_out[None, :], gate_up,
                             bsz=bsz, seq=seq, diff_w=diff_w, fox_w=fox_w)
    od, of = diff_out[0], fox_out[0]
    if down:
        down_b = diff_out[1:]
    if gate_up:
        gate_up_b = fox_out[1:]
    w_gate_b = gate_up_b[0].reshape(w_gate.shape)
    w_up_b = gate_up_b[1].reshape(w_up.shape)
    w_down_b = down_b[0].reshape(w_down.shape)

    w_out_b = w_out.astype(BF16)
    w_router_hi, w_router_lo = _split_bf16(jnp.pad(w_router, ((0, 0), (0, LANES - n_experts))))
    b_router_p = jnp.pad(b_router, (0, LANES - n_experts))[None, :]
    h, hn, logits = _outproj(od, of, w_out_b[:diff_w], w_out_b[diff_w:], x2, g_ffn_norm[None, :],
                             w_router_hi, w_router_lo, b_router_p)

    e_pad, gate_pad, rank_pad, cnt = _route(logits, n_experts)
    e_idx = e_pad[:, :TOP_K]
    counts = cnt[0, :n_experts].astype(jnp.int32)
    padded = (counts + EXPERT_TM - 1) // EXPERT_TM * EXPERT_TM
    pad_ends = jnp.cumsum(padded)
    pad_starts = pad_ends - padded
    start_of = jnp.sum(jnp.where(e_idx[..., None] == jnp.arange(n_experts), pad_starts, 0), axis=-1)
    dest = (start_of + rank_pad[:, :TOP_K]).astype(jnp.int32)
    n_tiles = -(-n * TOP_K // EXPERT_TM) + n_experts
    tile_start = jnp.arange(n_tiles, dtype=jnp.int32) * EXPERT_TM
    tile_expert = jnp.sum((pad_ends[None, :] <= tile_start[:, None]).astype(jnp.int32), axis=1)
    tile_expert = jnp.minimum(tile_expert, n_experts - 1)
    n_used = (pad_ends[-1:] // EXPERT_TM).astype(jnp.int32)
    tile_expert = jnp.where(jnp.arange(n_tiles) < n_used[0], tile_expert, tile_expert[n_used[0] - 1])
    last_tile = jnp.where(padded > 0, pad_ends // EXPERT_TM - 1, -1)
    tail_tile = n_used[0] + jnp.arange(n_tiles - n * TOP_K // EXPERT_TM, dtype=jnp.int32)
    tail_tile = jnp.where(tail_tile < n_tiles, tail_tile, -1)
    zero_tiles = jnp.concatenate([last_tile, tail_tile]).astype(jnp.int32)

    xs = _dispatch(zero_tiles, dest, hn, n_tiles * EXPERT_TM)
    ys = _experts(tile_expert, n_used, xs, w_gate_b, b_gate[:, None, :], w_up_b, b_up[:, None, :],
                  w_down_b, b_down[:, None, :])
    return dest, gate_pad, h, ys


def kernel(x, positions, g_attn_norm, w_in, b_forget, lambda_q1, lambda_k1, lambda_q2, lambda_k2, g_diff_sub,
           g_fox_out, w_out, g_ffn_norm, w_router, b_router, w_gate, b_gate, w_up, b_up, w_down, b_down,
           g_final):
    bsz, seq, d = x.shape
    depth = w_in.shape[0]
    cos, sin_signed = _rope_tables(positions)
    cos = cos.reshape(bsz * seq, HEAD_DIM)
    sin_signed = sin_signed.reshape(bsz * seq, HEAD_DIM)
    x2 = x.reshape(bsz * seq, d)
    for l in range(depth):
        dest, gates, h, ys = _layer(
            x2, cos, sin_signed, l, bsz, seq, g_attn_norm[l], w_in[l], b_forget[l], lambda_q1[l], lambda_k1[l],
            lambda_q2[l], lambda_k2[l], g_diff_sub[l], g_fox_out[l], w_out[l], g_ffn_norm[l], w_router[l],
            b_router[l], w_gate[l], b_gate[l], w_up[l], b_up[l], w_down[l], b_down[l])
        x2 = _combine(dest, gates, h, g_final[None, :], ys, final_norm=(l == depth - 1))
    return x2.reshape(bsz, seq, d)
```

```python
import functools
import math

import jax
import jax.numpy as jnp
from jax import lax
from jax.experimental import pallas as pl
from jax.experimental.pallas import tpu as pltpu

F32 = jnp.float32
BF16 = jnp.bfloat16

HEAD_DIM = 128
CHUNK = 64
ROPE_THETA = 10000.0
NORM_EPS = 1e-5
NEG_BIG = -1e30
TOP_K = 4
SWIGLU_LIMIT = 7.0
SWIGLU_ALPHA = 1.702
LANES = 128
QK_SCALE = HEAD_DIM ** -0.5
LOG2E = math.log2(math.e)
ONES_ROWS = 16

INPROJ_TM = 1024
INPROJ_TN = 512
ATTN_T = 512
VT_CHUNK = 512
OUTPROJ_TM = 512
ROUTE_TM = 512
DISPATCH_TM = 256
EXPERT_TM = 512
EXPERT_RT = 128
EXPERT_TC = 512
COMBINE_TM = 128
VMEM_LIMIT = 48 * 1024 * 1024


def _lambda_init(layer_idx):
    return 0.8 - 0.6 * math.exp(-0.3 * layer_idx)


def _cparams(sem):
    return pltpu.CompilerParams(dimension_semantics=sem, vmem_limit_bytes=VMEM_LIMIT)


def _split_bf16(x):
    hi = x.astype(BF16)
    return hi, (x - hi.astype(F32)).astype(BF16)


def _dot3(x_hi, x_lo, w_hi, w_lo):
    dot = functools.partial(jnp.dot, preferred_element_type=F32)
    return dot(x_hi, w_hi) + (dot(x_hi, w_lo) + dot(x_lo, w_hi))


def _inproj_body(x_ref, g_ref, w_ref, wfh_ref, wfl_ref, cos_ref, sin_ref, o_ref, fl_ref, xn_ref,
                 *, n_rope, tn, q_ranges):
    j = pl.program_id(1)

    @pl.when(j == 0)
    def _():
        x = x_ref[...]
        ms = jnp.mean(x * x, axis=-1, keepdims=True)
        xn = x * lax.rsqrt(ms + NORM_EPS) * g_ref[...]
        xn_hi, xn_lo = _split_bf16(xn)
        xn_ref[...] = xn_hi
        fl_ref[...] = _dot3(xn_hi, xn_lo, wfh_ref[...], wfl_ref[...])

    acc = jnp.dot(xn_ref[...], w_ref[...], preferred_element_type=F32)
    col0 = j * tn
    is_q = (col0 >= q_ranges[0][0]) & (col0 < q_ranges[0][1])
    is_q = is_q | ((col0 >= q_ranges[1][0]) & (col0 < q_ranges[1][1]))
    scale = jnp.where(is_q, QK_SCALE * LOG2E, 1.0).astype(F32)

    @pl.when(j < n_rope)
    def _():
        cos = cos_ref[...] * scale
        sin = sin_ref[...] * scale
        for gi in range(tn // LANES):
            xg = acc[:, gi * LANES:(gi + 1) * LANES]
            rot = pltpu.roll(xg, HEAD_DIM // 2, 1)
            o_ref[:, gi * LANES:(gi + 1) * LANES] = (xg * cos + rot * sin).astype(BF16)

    @pl.when(j >= n_rope)
    def _():
        o_ref[...] = (acc * scale).astype(BF16)


def _inproj(x2, g, w_main, w_f_hi, w_f_lo, cos, sin_signed, *, diff_w, fox_w):
    n, d = x2.shape
    cols = w_main.shape[1]
    tm, tn = INPROJ_TM, INPROJ_TN
    n_rope = 2 * diff_w // tn
    q_ranges = ((0, diff_w), (3 * diff_w, 3 * diff_w + fox_w))
    body = functools.partial(_inproj_body, n_rope=n_rope, tn=tn, q_ranges=q_ranges)
    return pl.pallas_call(
        body,
        out_shape=(jax.ShapeDtypeStruct((n, cols), BF16), jax.ShapeDtypeStruct((n, LANES), F32)),
        grid=(n // tm, cols // tn),
        in_specs=[
            pl.BlockSpec((tm, d), lambda i, j: (i, 0)),
            pl.BlockSpec((1, d), lambda i, j: (0, 0)),
            pl.BlockSpec((d, tn), lambda i, j: (0, j)),
            pl.BlockSpec((d, LANES), lambda i, j: (0, 0)),
            pl.BlockSpec((d, LANES), lambda i, j: (0, 0)),
            pl.BlockSpec((tm, LANES), lambda i, j: (i, 0)),
            pl.BlockSpec((tm, LANES), lambda i, j: (i, 0)),
        ],
        out_specs=(pl.BlockSpec((tm, tn), lambda i, j: (i, j)),
                   pl.BlockSpec((tm, LANES), lambda i, j: (i, 0))),
        scratch_shapes=[pltpu.VMEM((tm, d), BF16)],
        compiler_params=_cparams(("parallel", "arbitrary")),
        name="inproj",
    )(x2, g, w_main, w_f_hi, w_f_lo, cos, sin_signed)


def _cumgate_body(z_ref, b_ref, c_ref):
    z = z_ref[...] + b_ref[...]
    v = jnp.minimum(z, 0.0) - jnp.log(1.0 + jnp.exp(-jnp.abs(z)))
    seq = v.shape[1]
    lane = lax.broadcasted_iota(jnp.int32, v.shape, 1)
    shift = 1
    while shift < seq:
        v = v + jnp.where(lane >= shift, pltpu.roll(v, shift, 1), 0.0)
        shift *= 2
    c_ref[...] = v


def _cumgate(z_rows, b_rows):
    return pl.pallas_call(
        _cumgate_body,
        out_shape=jax.ShapeDtypeStruct(z_rows.shape, F32),
        name="cumgate",
    )(z_rows, b_rows)


def _dot_nt(a, b):
    return lax.dot_general(a, b, (((1,), (1,)), ((), ())), preferred_element_type=F32)


def _build_vt(v_ref, vt_ref):
    seq, dv = v_ref.shape
    for c in range(seq // VT_CHUNK):
        blk = v_ref[c * VT_CHUNK:(c + 1) * VT_CHUNK, :].astype(F32)
        vt_ref[0:dv, c * VT_CHUNK:(c + 1) * VT_CHUNK] = blk.T.astype(BF16)
    vt_ref[dv:dv + ONES_ROWS, :] = jnp.ones((ONES_ROWS, seq), BF16)


def _softmax_step(s_t, vt_blk, m_ref, acc_ref):
    m_prev = m_ref[...]
    m_new = jnp.maximum(m_prev, jnp.max(s_t, axis=0, keepdims=True))
    alpha = jnp.exp2(m_prev - m_new)
    p_t = jnp.exp2(s_t - m_new).astype(BF16)
    acc_ref[...] = alpha * acc_ref[...] + jnp.dot(vt_blk, p_t, preferred_element_type=F32)
    m_ref[...] = m_new


def _sweep_keys(qi, t, step):
    def pair(i, carry):
        step(pl.multiple_of(i * (2 * t), 2 * t), 2 * t, False)
        return carry

    lax.fori_loop(0, qi // 2, pair, 0)

    @pl.when(qi % 2 == 1)
    def _():
        step(pl.multiple_of((qi - 1) * t, t), t, False)

    step(pl.multiple_of(qi * t, t), t, True)


def _normalised(acc, dv):
    return acc[0:dv] * (1.0 / acc[dv:dv + 1])


def _cast_rows_specs(weights, steps, step_of):
    in_specs, out_specs, out_shapes = [], [], []
    for w in weights:
        rows, cols = w.shape
        spec = pl.BlockSpec((rows // steps, cols), lambda b, h, i: (step_of(b, h, i), 0))
        in_specs.append(spec)
        out_specs.append(spec)
        out_shapes.append(jax.ShapeDtypeStruct((rows, cols), BF16))
    return in_specs, out_specs, out_shapes


def _can_cast_rows(weights, steps):
    return all(w.shape[0] % steps == 0 and (w.shape[0] // steps) % 16 == 0 for w in weights)


def _cast_rows(in_refs, out_refs):
    for src, dst in zip(in_refs, out_refs):
        dst[...] = src[...].astype(BF16)


def _diff_body(lq1_ref, lk1_ref, lq2_ref, lk2_ref, gsub_ref, q1_ref, q2_ref, k1_ref, k2_ref, v_ref,
               *rest, t, lam_init, n_cast):
    cast_in, (o_ref, *cast_out), (vt, m1, a1, m2, a2) = rest[:n_cast], rest[n_cast:2 * n_cast + 1], rest[2 * n_cast + 1:]
    _cast_rows(cast_in, cast_out)
    qi = pl.program_id(2)
    dv = v_ref.shape[1]

    @pl.when(qi == 0)
    def _():
        _build_vt(v_ref, vt)

    for m_ref, a_ref in ((m1, a1), (m2, a2)):
        m_ref[...] = jnp.full(m_ref.shape, NEG_BIG, F32)
        a_ref[...] = jnp.zeros(a_ref.shape, F32)
    q1 = q1_ref[...]
    q2 = q2_ref[...]
    key_chunk = lax.broadcasted_iota(jnp.int32, (t, t), 0) // CHUNK
    qry_chunk = lax.broadcasted_iota(jnp.int32, (t, t), 1) // CHUNK
    visible = key_chunk <= qry_chunk

    def step(off, size, diagonal):
        vt_blk = vt[:, pl.ds(off, size)]
        for q, k_ref, m_ref, a_ref in ((q1, k1_ref, m1, a1), (q2, k2_ref, m2, a2)):
            s_t = _dot_nt(k_ref[pl.ds(off, size), :], q)
            if diagonal:
                s_t = jnp.where(visible, s_t, NEG_BIG)
            _softmax_step(s_t, vt_blk, m_ref, a_ref)

    _sweep_keys(qi, t, step)

    lam = (jnp.exp(jnp.sum(lq1_ref[...] * lk1_ref[...], axis=-1, keepdims=True))
           - jnp.exp(jnp.sum(lq2_ref[...] * lk2_ref[...], axis=-1, keepdims=True)) + lam_init)
    o = (_normalised(a1[...], dv) - lam * _normalised(a2[...], dv)).T
    ms = jnp.mean(o * o, axis=-1, keepdims=True)
    y = o * lax.rsqrt(ms + NORM_EPS) * gsub_ref[...]
    o_ref[...] = (y * (1.0 - lam_init)).astype(BF16)


def _diff_attention(proj, lq1, lk1, lq2, lk2, gsub, cast_weights, *, bsz, seq, diff_w, lam_init):
    t = ATTN_T
    dv = 2 * HEAD_DIM
    n_heads = diff_w // dv
    nq = seq // t
    kcol = diff_w // HEAD_DIM
    vcol = 2 * diff_w // dv
    vec = pl.BlockSpec((1, HEAD_DIM), lambda b, h, i: (0, 0))
    cast_in, cast_out, cast_shapes = _cast_rows_specs(
        cast_weights, bsz * n_heads * nq, lambda b, h, i: (b * n_heads + h) * nq + i)
    body = functools.partial(_diff_body, t=t, lam_init=lam_init, n_cast=len(cast_weights))
    return pl.pallas_call(
        body,
        out_shape=(jax.ShapeDtypeStruct((bsz * seq, diff_w), BF16), *cast_shapes),
        grid=(bsz, n_heads, nq),
        in_specs=[
            vec, vec, vec, vec,
            pl.BlockSpec((1, dv), lambda b, h, i: (0, 0)),
            pl.BlockSpec((t, HEAD_DIM), lambda b, h, i: (b * nq + i, 2 * h)),
            pl.BlockSpec((t, HEAD_DIM), lambda b, h, i: (b * nq + i, 2 * h + 1)),
            pl.BlockSpec((seq, HEAD_DIM), lambda b, h, i: (b, kcol + 2 * h)),
            pl.BlockSpec((seq, HEAD_DIM), lambda b, h, i: (b, kcol + 2 * h + 1)),
            pl.BlockSpec((seq, dv), lambda b, h, i: (b, vcol + h)),
            *cast_in,
        ],
        out_specs=(pl.BlockSpec((t, dv), lambda b, h, i: (b * nq + i, h)), *cast_out),
        scratch_shapes=[pltpu.VMEM((dv + ONES_ROWS, seq), BF16),
                        pltpu.VMEM((1, t), F32), pltpu.VMEM((dv + ONES_ROWS, t), F32),
                        pltpu.VMEM((1, t), F32), pltpu.VMEM((dv + ONES_ROWS, t), F32)],
        compiler_params=_cparams(("arbitrary", "arbitrary", "arbitrary")),
        name="diffattn",
    )(lq1, lk1, lq2, lk2, gsub, proj, proj, proj, proj, proj, *cast_weights)


def _fox_body(g_ref, q_ref, k_ref, ck_ref, v_ref, *rest, t, n_cast):
    cast_in, (o_ref, *cast_out), (vt, ck_lanes, m, acc) = rest[:n_cast], rest[n_cast:2 * n_cast + 1], rest[2 * n_cast + 1:]
    _cast_rows(cast_in, cast_out)
    qi = pl.program_id(2)
    dv = v_ref.shape[1]

    @pl.when(qi == 0)
    def _():
        _build_vt(v_ref, vt)
        ck_lanes[...] = jnp.broadcast_to(ck_ref[...], ck_lanes.shape)

    m[...] = jnp.full(m.shape, NEG_BIG, F32)
    acc[...] = jnp.zeros(acc.shape, F32)
    q = q_ref[...]
    causal = lax.broadcasted_iota(jnp.int32, (t, t), 0) <= lax.broadcasted_iota(jnp.int32, (t, t), 1)

    def step(off, size, diagonal):
        ck = ck_lanes[pl.ds(off, size), :]
        s_t = _dot_nt(k_ref[pl.ds(off, size), :], q) - jnp.concatenate([ck] * (t // LANES), axis=1)
        if diagonal:
            s_t = jnp.where(causal, s_t, NEG_BIG)
        _softmax_step(s_t, vt[:, pl.ds(off, size)], m, acc)

    _sweep_keys(qi, t, step)

    o = _normalised(acc[...], dv).T
    ms = jnp.mean(o * o, axis=-1, keepdims=True)
    o_ref[...] = (o * lax.rsqrt(ms + NORM_EPS) * g_ref[...]).astype(BF16)


def _fox_attention(proj, c_cols, g_fox, cast_weights, *, bsz, seq, diff_w, fox_w):
    t = ATTN_T
    n_heads = fox_w // HEAD_DIM
    nq = seq // t
    qcol = 3 * diff_w // HEAD_DIM
    kcol = qcol + n_heads
    vcol = kcol + n_heads
    cast_in, cast_out, cast_shapes = _cast_rows_specs(
        cast_weights, bsz * n_heads * nq, lambda b, h, i: (b * n_heads + h) * nq + i)
    body = functools.partial(_fox_body, t=t, n_cast=len(cast_weights))
    return pl.pallas_call(
        body,
        out_shape=(jax.ShapeDtypeStruct((bsz * seq, fox_w), BF16), *cast_shapes),
        grid=(bsz, n_heads, nq),
        in_specs=[
            pl.BlockSpec((1, HEAD_DIM), lambda b, h, i: (0, 0)),
            pl.BlockSpec((t, HEAD_DIM), lambda b, h, i: (b * nq + i, qcol + h)),
            pl.BlockSpec((seq, HEAD_DIM), lambda b, h, i: (b, kcol + h)),
            pl.BlockSpec((None, seq, 1), lambda b, h, i: (b * n_heads + h, 0, 0)),
            pl.BlockSpec((seq, HEAD_DIM), lambda b, h, i: (b, vcol + h)),
            *cast_in,
        ],
        out_specs=(pl.BlockSpec((t, HEAD_DIM), lambda b, h, i: (b * nq + i, h)), *cast_out),
        scratch_shapes=[pltpu.VMEM((HEAD_DIM + ONES_ROWS, seq), BF16), pltpu.VMEM((seq, LANES), F32),
                        pltpu.VMEM((1, t), F32), pltpu.VMEM((HEAD_DIM + ONES_ROWS, t), F32)],
        compiler_params=_cparams(("arbitrary", "arbitrary", "arbitrary")),
        name="foxattn",
    )(g_fox, proj, proj, c_cols, proj, *cast_weights)


def _outproj_body(od_ref, of_ref, wd_ref, wf_ref, x_ref, g_ref, wrh_ref, wrl_ref, br_ref, h_ref, hn_ref, lg_ref):
    h = (x_ref[...]
         + jnp.dot(od_ref[...], wd_ref[...], preferred_element_type=F32)
         + jnp.dot(of_ref[...], wf_ref[...], preferred_element_type=F32))
    h_ref[...] = h
    ms = jnp.mean(h * h, axis=-1, keepdims=True)
    hn = h * lax.rsqrt(ms + NORM_EPS) * g_ref[...]
    hn_ref[...] = hn
    hn_hi, hn_lo = _split_bf16(hn)
    lg_ref[...] = _dot3(hn_hi, hn_lo, wrh_ref[...], wrl_ref[...]) + br_ref[...]


def _outproj(od, of, w_od, w_of, x2, g_ffn, w_router_hi, w_router_lo, b_router_p):
    n, d = x2.shape
    tm = OUTPROJ_TM
    row = lambda i: (i, 0)
    const = lambda i: (0, 0)
    resident = functools.partial(pl.BlockSpec, index_map=const, pipeline_mode=pl.Buffered(1))
    return pl.pallas_call(
        _outproj_body,
        out_shape=(jax.ShapeDtypeStruct((n, d), F32), jax.ShapeDtypeStruct((n, d), F32),
                   jax.ShapeDtypeStruct((n, LANES), F32)),
        grid=(n // tm,),
        in_specs=[
            pl.BlockSpec((tm, od.shape[1]), row),
            pl.BlockSpec((tm, of.shape[1]), row),
            resident(w_od.shape),
            resident(w_of.shape),
            pl.BlockSpec((tm, d), row),
            pl.BlockSpec((1, d), const),
            resident((d, LANES)),
            resident((d, LANES)),
            pl.BlockSpec((1, LANES), const),
        ],
        out_specs=(pl.BlockSpec((tm, d), row), pl.BlockSpec((tm, d), row), pl.BlockSpec((tm, LANES), row)),
        compiler_params=_cparams(("parallel",)),
        name="outproj",
    )(od, of, w_od, w_of, x2, g_ffn, w_router_hi, w_router_lo, b_router_p)


def _route_body(lg_ref, e_ref, gate_ref, rank_ref, cnt_ref, carry_ref, *, n_experts):
    i = pl.program_id(0)
    tm = lg_ref.shape[0]

    @pl.when(i == 0)
    def _():
        carry_ref[...] = jnp.zeros(carry_ref.shape, F32)

    lane = lax.broadcasted_iota(jnp.int32, (tm, LANES), 1)
    logits = jnp.where(lane < n_experts, lg_ref[...], -jnp.inf)
    vals, idxs, hots = [], [], []
    for _ in range(TOP_K):
        mx = jnp.max(logits, axis=-1, keepdims=True)
        idx = jnp.min(jnp.where(logits == mx, lane, LANES), axis=-1, keepdims=True)
        hot = lane == idx
        logits = jnp.where(hot, -jnp.inf, logits)
        vals.append(mx)
        idxs.append(idx)
        hots.append(hot)

    exps = [jnp.exp(v - vals[0]) for v in vals]
    denom = exps[0]
    for ex in exps[1:]:
        denom = denom + ex
    inv = 1.0 / denom

    hot_sum = jnp.zeros((tm, LANES), F32)
    for hot in hots:
        hot_sum = hot_sum + jnp.where(hot, 1.0, 0.0)
    r = lax.broadcasted_iota(jnp.int32, (tm, tm), 0)
    c = lax.broadcasted_iota(jnp.int32, (tm, tm), 1)
    earlier = jnp.where(c < r, 1.0, 0.0).astype(BF16)
    base = carry_ref[...] + jnp.dot(earlier, hot_sum.astype(BF16), preferred_element_type=F32)

    e_out = jnp.zeros((tm, LANES), jnp.int32)
    g_out = jnp.zeros((tm, LANES), F32)
    r_out = jnp.zeros((tm, LANES), jnp.int32)
    for k in range(TOP_K):
        rank = jnp.sum(jnp.where(hots[k], base, 0.0), axis=-1, keepdims=True).astype(jnp.int32)
        e_out = jnp.where(lane == k, idxs[k], e_out)
        g_out = jnp.where(lane == k, exps[k] * inv, g_out)
        r_out = jnp.where(lane == k, rank, r_out)
    e_ref[...] = e_out
    gate_ref[...] = g_out
    rank_ref[...] = r_out
    carry_ref[...] = carry_ref[...] + jnp.sum(hot_sum, axis=0, keepdims=True)
    cnt_ref[...] = carry_ref[...]


def _route(logits, n_experts):
    n = logits.shape[0]
    tm = ROUTE_TM
    row = lambda i: (i, 0)
    body = functools.partial(_route_body, n_experts=n_experts)
    return pl.pallas_call(
        body,
        out_shape=(jax.ShapeDtypeStruct((n, LANES), jnp.int32), jax.ShapeDtypeStruct((n, LANES), F32),
                   jax.ShapeDtypeStruct((n, LANES), jnp.int32), jax.ShapeDtypeStruct((1, LANES), F32)),
        grid=(n // tm,),
        in_specs=[pl.BlockSpec((tm, LANES), row)],
        out_specs=(pl.BlockSpec((tm, LANES), row), pl.BlockSpec((tm, LANES), row),
                   pl.BlockSpec((tm, LANES), row), pl.BlockSpec((1, LANES), lambda i: (0, 0))),
        scratch_shapes=[pltpu.VMEM((1, LANES), F32)],
        compiler_params=_cparams(("arbitrary",)),
        name="route",
    )(logits)


def _dispatch_body(ztile_ref, dest_ref, hn_ref, xs_ref, zeros, sem, zsem):
    tm = hn_ref.shape[0]

    @pl.when(pl.program_id(0) == 0)
    def _():
        zeros[...] = jnp.zeros(zeros.shape, zeros.dtype)

        def fill(z):
            row0 = pl.multiple_of(ztile_ref[z] * EXPERT_TM, EXPERT_TM)
            return pltpu.make_async_copy(zeros, xs_ref.at[pl.ds(row0, EXPERT_TM), :], zsem)

        def fill_start(z, carry):
            @pl.when(ztile_ref[z] >= 0)
            def _():
                fill(z).start()
            return carry

        def fill_wait(z, carry):
            @pl.when(ztile_ref[z] >= 0)
            def _():
                fill(z).wait()
            return carry

        lax.fori_loop(0, ztile_ref.shape[0], fill_start, 0)
        lax.fori_loop(0, ztile_ref.shape[0], fill_wait, 0)

    def start(t, carry):
        src = hn_ref.at[pl.ds(t, 1), :]
        for k in range(TOP_K):
            pltpu.make_async_copy(src, xs_ref.at[pl.ds(dest_ref[0, t * TOP_K + k], 1), :], sem).start()
        return carry

    lax.fori_loop(0, tm, start, 0)
    for _ in range(TOP_K):
        pltpu.make_async_copy(hn_ref, xs_ref.at[pl.ds(0, tm), :], sem).wait()


def _dispatch(zero_tiles, dest, hn, n_rows):
    n, d = hn.shape
    tm = DISPATCH_TM
    dest3 = dest.reshape(n // tm, 1, tm * TOP_K)
    return pl.pallas_call(
        _dispatch_body,
        out_shape=jax.ShapeDtypeStruct((n_rows, d), hn.dtype),
        grid_spec=pltpu.PrefetchScalarGridSpec(
            num_scalar_prefetch=1,
            grid=(n // tm,),
            in_specs=[
                pl.BlockSpec((None, 1, tm * TOP_K), lambda i, zt: (i, 0, 0), memory_space=pltpu.SMEM),
                pl.BlockSpec((tm, d), lambda i, zt: (i, 0)),
            ],
            out_specs=pl.BlockSpec(memory_space=pl.ANY),
            scratch_shapes=[pltpu.VMEM((EXPERT_TM, d), hn.dtype), pltpu.SemaphoreType.DMA,
                            pltpu.SemaphoreType.DMA],
        ),
        compiler_params=_cparams(("arbitrary",)),
        name="dispatch",
    )(zero_tiles, dest3, hn)


def _experts_body(te_ref, nu_ref, live_ref, x_ref, wg_ref, bg_ref, wu_ref, bu_ref, wd_ref, bd_ref, o_ref, xb_ref):
    i = pl.program_id(0)
    j = pl.program_id(1)
    tm = x_ref.shape[0]

    def compute(r):
        @pl.when(j == 0)
        def _():
            xb_ref[0:r, :] = x_ref[0:r, :].astype(BF16)
            o_ref[0:r, :] = jnp.broadcast_to(bd_ref[...], (r, o_ref.shape[1]))
            if r < tm:
                o_ref[r:tm, :] = jnp.zeros((tm - r, o_ref.shape[1]), F32)

        xb = xb_ref[0:r, :]
        g = jnp.dot(xb, wg_ref[...], preferred_element_type=F32) + bg_ref[...]
        u = jnp.dot(xb, wu_ref[...], preferred_element_type=F32) + bu_ref[...]
        g = jnp.minimum(g, SWIGLU_LIMIT)
        u = jnp.clip(u, -SWIGLU_LIMIT, SWIGLU_LIMIT)
        hdn = g * (1.0 / (1.0 + jnp.exp(-SWIGLU_ALPHA * g))) * (u + 1.0)
        o_ref[0:r, :] += jnp.dot(hdn.astype(BF16), wd_ref[...], preferred_element_type=F32)

    for r in range(EXPERT_RT, tm + 1, EXPERT_RT):
        pl.when(live_ref[i] == r)(functools.partial(compute, r))

    @pl.when((live_ref[i] == 0) & (j == 0))
    def _():
        o_ref[...] = jnp.zeros(o_ref.shape, F32)


def _experts(tile_expert, n_used, tile_live, xs, wg, bg, wu, bu, wd, bd):
    rows, d = xs.shape
    d_exp = wg.shape[2]
    tm, tc = EXPERT_TM, EXPERT_TC
    n_tiles = rows // tm
    nj = d_exp // tc

    def tile(i, nu):
        return jnp.minimum(i, nu[0] - 1)

    def chunk(i, j, nu):
        return jnp.where(i < nu[0], j, nj - 1)

    return pl.pallas_call(
        _experts_body,
        out_shape=jax.ShapeDtypeStruct((rows, d), F32),
        grid_spec=pltpu.PrefetchScalarGridSpec(
            num_scalar_prefetch=3,
            grid=(n_tiles, nj),
            in_specs=[
                pl.BlockSpec((tm, d), lambda i, j, te, nu, live: (tile(i, nu), 0)),
                pl.BlockSpec((None, d, tc), lambda i, j, te, nu, live: (te[i], 0, chunk(i, j, nu))),
                pl.BlockSpec((None, 1, tc), lambda i, j, te, nu, live: (te[i], 0, chunk(i, j, nu))),
                pl.BlockSpec((None, d, tc), lambda i, j, te, nu, live: (te[i], 0, chunk(i, j, nu))),
                pl.BlockSpec((None, 1, tc), lambda i, j, te, nu, live: (te[i], 0, chunk(i, j, nu))),
                pl.BlockSpec((None, tc, d), lambda i, j, te, nu, live: (te[i], chunk(i, j, nu), 0)),
                pl.BlockSpec((None, 1, d), lambda i, j, te, nu, live: (te[i], 0, 0)),
            ],
            out_specs=pl.BlockSpec((tm, d), lambda i, j, te, nu, live: (i, 0)),
            scratch_shapes=[pltpu.VMEM((tm, d), BF16)],
        ),
        compiler_params=_cparams(("arbitrary", "arbitrary")),
        name="experts",
    )(tile_expert, n_used, tile_live, xs, wg, bg, wu, bu, wd, bd)


def _combine_body(dcur_ref, dnext_ref, gate_ref, h_ref, gfin_ref, ys_ref, o_ref, buf, sems, *, final_norm):
    i = pl.program_id(0)
    nb = pl.num_programs(0)
    tm = h_ref.shape[0]
    slot = i % 2

    def start_all(dref, s):
        def go(t, carry):
            for k in range(TOP_K):
                pltpu.make_async_copy(ys_ref.at[pl.ds(dref[0, t * TOP_K + k], 1), :],
                                      buf.at[s, k, pl.ds(t, 1), :], sems.at[s]).start()
            return carry
        lax.fori_loop(0, tm, go, 0)

    @pl.when(i == 0)
    def _():
        start_all(dcur_ref, 0)

    @pl.when(i + 1 < nb)
    def _():
        start_all(dnext_ref, 1 - slot)

    for k in range(TOP_K):
        pltpu.make_async_copy(ys_ref.at[pl.ds(0, tm), :], buf.at[slot, k], sems.at[slot]).wait()

    y = h_ref[...]
    gates = gate_ref[...]
    for k in range(TOP_K):
        y = y + gates[:, k:k + 1] * buf[slot, k]
    if final_norm:
        ms = jnp.mean(y * y, axis=-1, keepdims=True)
        y = y * lax.rsqrt(ms + NORM_EPS) * gfin_ref[...]
    o_ref[...] = y


def _combine(dest, gates, h, g_final, ys, *, final_norm):
    n, d = h.shape
    tm = COMBINE_TM
    nb = n // tm
    dest3 = dest.reshape(nb, 1, tm * TOP_K)
    smem_blk = (None, 1, tm * TOP_K)
    return pl.pallas_call(
        functools.partial(_combine_body, final_norm=final_norm),
        out_shape=jax.ShapeDtypeStruct((n, d), F32),
        grid=(nb,),
        in_specs=[
            pl.BlockSpec(smem_blk, lambda i: (i, 0, 0), memory_space=pltpu.SMEM),
            pl.BlockSpec(smem_blk, lambda i: (jnp.minimum(i + 1, nb - 1), 0, 0), memory_space=pltpu.SMEM),
            pl.BlockSpec((tm, LANES), lambda i: (i, 0)),
            pl.BlockSpec((tm, d), lambda i: (i, 0)),
            pl.BlockSpec((1, d), lambda i: (0, 0)),
            pl.BlockSpec(memory_space=pl.ANY),
        ],
        out_specs=pl.BlockSpec((tm, d), lambda i: (i, 0)),
        scratch_shapes=[pltpu.VMEM((2, TOP_K, tm, d), F32), pltpu.SemaphoreType.DMA((2,))],
        compiler_params=_cparams(("arbitrary",)),
        name="combine",
    )(dest3, dest3, gates, h, g_final, ys)


def _rope_tables(positions):
    inv_freq = ROPE_THETA ** (-jnp.arange(0, HEAD_DIM, 2, dtype=F32) / HEAD_DIM)
    ang = positions.astype(F32)[..., None] * inv_freq
    ang = jnp.concatenate([ang, ang], axis=-1)
    sign = jnp.concatenate([-jnp.ones((HEAD_DIM // 2,), F32), jnp.ones((HEAD_DIM // 2,), F32)])
    return jnp.cos(ang), jnp.sin(ang) * sign


def _layer(x2, cos, sin_signed, layer_idx, bsz, seq, g_attn_norm, w_in, b_forget, lambda_q1, lambda_k1,
           lambda_q2, lambda_k2, g_diff_sub, g_fox_out, w_out, g_ffn_norm, w_router, b_router,
           w_gate, b_gate, w_up, b_up, w_down, b_down):
    n, d = x2.shape
    diff_w = d // 2
    fox_w = d - diff_w
    fox_heads = fox_w // HEAD_DIM
    main_cols = 3 * diff_w + 3 * fox_w
    n_experts = w_router.shape[1]
    lam_init = _lambda_init(layer_idx)

    w_main = w_in[:, :main_cols].astype(BF16)
    w_f_hi, w_f_lo = _split_bf16(jnp.pad(w_in[:, main_cols:], ((0, 0), (0, LANES - fox_heads))))
    proj, f_logit = _inproj(x2, g_attn_norm[None, :], w_main, w_f_hi, w_f_lo, cos, sin_signed,
                            diff_w=diff_w, fox_w=fox_w)

    z_rows = f_logit[:, :fox_heads].reshape(bsz, seq, fox_heads).transpose(0, 2, 1).reshape(bsz * fox_heads, seq)
    b_rows = jnp.tile(b_forget.astype(F32), bsz)[:, None]
    c_rows = _cumgate(z_rows, b_rows)

    d_exp = w_gate.shape[2]
    gate_up = (w_gate.reshape(n_experts * d, d_exp), w_up.reshape(n_experts * d, d_exp))
    down = (w_down.reshape(n_experts * d_exp, d),)
    diff_steps = bsz * (diff_w // (2 * HEAD_DIM)) * (seq // ATTN_T)
    fox_steps = bsz * fox_heads * (seq // ATTN_T)
    if not _can_cast_rows(down, diff_steps):
        down_b, down = tuple(w.astype(BF16) for w in down), ()
    if not _can_cast_rows(gate_up, fox_steps):
        gate_up_b, gate_up = tuple(w.astype(BF16) for w in gate_up), ()

    diff_out = _diff_attention(proj, lambda_q1[None, :], lambda_k1[None, :], lambda_q2[None, :],
                               lambda_k2[None, :], g_diff_sub[None, :], down,
                               bsz=bsz, seq=seq, diff_w=diff_w, lam_init=lam_init)
    fox_out = _fox_attention(proj, (c_rows * LOG2E)[:, :, None], g_fox_out[None, :], gate_up,
                             bsz=bsz, seq=seq, diff_w=diff_w, fox_w=fox_w)
    od, of = diff_out[0], fox_out[0]
    if down:
        down_b = diff_out[1:]
    if gate_up:
        gate_up_b = fox_out[1:]
    w_gate_b = gate_up_b[0].reshape(w_gate.shape)
    w_up_b = gate_up_b[1].reshape(w_up.shape)
    w_down_b = down_b[0].reshape(w_down.shape)

    w_out_b = w_out.astype(BF16)
    w_router_hi, w_router_lo = _split_bf16(jnp.pad(w_router, ((0, 0), (0, LANES - n_experts))))
    b_router_p = jnp.pad(b_router, (0, LANES - n_experts))[None, :]
    h, hn, logits = _outproj(od, of, w_out_b[:diff_w], w_out_b[diff_w:], x2, g_ffn_norm[None, :],
                             w_router_hi, w_router_lo, b_router_p)

    e_pad, gate_pad, rank_pad, cnt = _route(logits, n_experts)
    e_idx = e_pad[:, :TOP_K]
    counts = cnt[0, :n_experts].astype(jnp.int32)
    padded = (counts + EXPERT_TM - 1) // EXPERT_TM * EXPERT_TM
    pad_ends = jnp.cumsum(padded)
    pad_starts = pad_ends - padded
    start_of = jnp.sum(jnp.where(e_idx[..., None] == jnp.arange(n_experts), pad_starts, 0), axis=-1)
    dest = (start_of + rank_pad[:, :TOP_K]).astype(jnp.int32)
    n_tiles = -(-n * TOP_K // EXPERT_TM) + n_experts
    tile_start = jnp.arange(n_tiles, dtype=jnp.int32) * EXPERT_TM
    tile_expert = jnp.sum((pad_ends[None, :] <= tile_start[:, None]).astype(jnp.int32), axis=1)
    tile_expert = jnp.minimum(tile_expert, n_experts - 1)
    n_used = (pad_ends[-1:] // EXPERT_TM).astype(jnp.int32)
    tile_expert = jnp.where(jnp.arange(n_tiles) < n_used[0], tile_expert, tile_expert[n_used[0] - 1])
    last_tile = jnp.where(padded > 0, pad_ends // EXPERT_TM - 1, -1)
    tail_tile = n_used[0] + jnp.arange(n_tiles - n * TOP_K // EXPERT_TM, dtype=jnp.int32)
    tail_tile = jnp.where(tail_tile < n_tiles, tail_tile, -1)
    zero_tiles = jnp.concatenate([last_tile, tail_tile]).astype(jnp.int32)

    routed_end = (pad_starts + counts)[tile_expert]
    tile_live = jnp.clip(routed_end - tile_start, 0, EXPERT_TM)
    tile_live = jnp.where(jnp.arange(n_tiles) < n_used[0], tile_live, 0)
    tile_live = ((tile_live + EXPERT_RT - 1) // EXPERT_RT * EXPERT_RT).astype(jnp.int32)

    xs = _dispatch(zero_tiles, dest, hn, n_tiles * EXPERT_TM)
    ys = _experts(tile_expert, n_used, tile_live, xs, w_gate_b, b_gate[:, None, :], w_up_b, b_up[:, None, :],
                  w_down_b, b_down[:, None, :])
    return dest, gate_pad, h, ys


def kernel(x, positions, g_attn_norm, w_in, b_forget, lambda_q1, lambda_k1, lambda_q2, lambda_k2, g_diff_sub,
           g_fox_out, w_out, g_ffn_norm, w_router, b_router, w_gate, b_gate, w_up, b_up, w_down, b_down,
           g_final):
    bsz, seq, d = x.shape
    depth = w_in.shape[0]
    cos, sin_signed = _rope_tables(positions)
    cos = cos.reshape(bsz * seq, HEAD_DIM)
    sin_signed = sin_signed.reshape(bsz * seq, HEAD_DIM)
    x2 = x.reshape(bsz * seq, d)
    for l in range(depth):
        dest, gates, h, ys = _layer(
            x2, cos, sin_signed, l, bsz, seq, g_attn_norm[l], w_in[l], b_forget[l], lambda_q1[l], lambda_k1[l],
            lambda_q2[l], lambda_k2[l], g_diff_sub[l], g_fox_out[l], w_out[l], g_ffn_norm[l], w_router[l],
            b_router[l], w_gate[l], b_gate[l], w_up[l], b_up[l], w_down[l], b_down[l])
        x2 = _combine(dest, gates, h, g_final[None, :], ys, final_norm=(l == depth - 1))
    return x2.reshape(bsz, seq, d)
```

```python
import functools
import math

import jax
import jax.numpy as jnp
from jax import lax
from jax.experimental import pallas as pl
from jax.experimental.pallas import tpu as pltpu

F32 = jnp.float32
BF16 = jnp.bfloat16

HEAD_DIM = 128
CHUNK = 64
ROPE_THETA = 10000.0
NORM_EPS = 1e-5
NEG_BIG = -1e30
TOP_K = 4
SWIGLU_LIMIT = 7.0
SWIGLU_ALPHA = 1.702
LANES = 128
QK_SCALE = HEAD_DIM ** -0.5
LOG2E = math.log2(math.e)
ONES_ROWS = 16

INPROJ_TM = 1024
INPROJ_TN = 512
DIFF_T = 512
FOX_T = 1024
CAST_ROWS = 512
VT_CHUNK = 512
OUTPROJ_TM = 512
ROUTE_TM = 512
DISPATCH_TM = 256
EXPERT_TM = 512
EXPERT_RT = 128
EXPERT_TC = 512
COMBINE_TM = 128
VMEM_LIMIT = 48 * 1024 * 1024
BIG_VMEM_LIMIT = 58 * 1024 * 1024


def _lambda_init(layer_idx):
    return 0.8 - 0.6 * math.exp(-0.3 * layer_idx)


def _cparams(sem, vmem_limit=VMEM_LIMIT):
    return pltpu.CompilerParams(dimension_semantics=sem, vmem_limit_bytes=vmem_limit)


def _split_bf16(x):
    hi = x.astype(BF16)
    return hi, (x - hi.astype(F32)).astype(BF16)


def _dot3(x_hi, x_lo, w_hi, w_lo):
    dot = functools.partial(jnp.dot, preferred_element_type=F32)
    return dot(x_hi, w_hi) + (dot(x_hi, w_lo) + dot(x_lo, w_hi))


def _cast_rows_specs(weights, step_of):
    in_specs, out_specs, out_shapes = [], [], []
    for w in weights:
        rows, cols = w.shape
        last = rows // CAST_ROWS - 1
        spec = pl.BlockSpec((CAST_ROWS, cols), lambda *idx, last=last: (jnp.minimum(step_of(*idx), last), 0))
        in_specs.append(spec)
        out_specs.append(spec)
        out_shapes.append(jax.ShapeDtypeStruct((rows, cols), BF16))
    return in_specs, out_specs, out_shapes


def _can_cast_rows(weights, steps):
    return all(w.shape[0] % CAST_ROWS == 0 and w.shape[0] // CAST_ROWS <= steps for w in weights)


def _linear_step(n_axes):
    step = pl.program_id(0)
    for axis in range(1, n_axes):
        step = step * pl.num_programs(axis) + pl.program_id(axis)
    return step


def _cast_rows(step, in_refs, out_refs, total_rows):
    for src, dst, rows in zip(in_refs, out_refs, total_rows):
        @pl.when(step < rows // CAST_ROWS)
        def _():
            dst[...] = src[...].astype(BF16)


def _inproj_body(x_ref, g_ref, w_ref, wfh_ref, wfl_ref, cos_ref, sin_ref, *rest, n_rope, tn, q_ranges, cast_rows):
    n_cast = len(cast_rows)
    cast_in, (o_ref, fl_ref, *cast_out), (xn_ref,) = rest[:n_cast], rest[n_cast:2 * n_cast + 2], rest[2 * n_cast + 2:]
    _cast_rows(_linear_step(2), cast_in, cast_out, cast_rows)
    j = pl.program_id(1)

    @pl.when(j == 0)
    def _():
        x = x_ref[...]
        ms = jnp.mean(x * x, axis=-1, keepdims=True)
        xn = x * lax.rsqrt(ms + NORM_EPS) * g_ref[...]
        xn_hi, xn_lo = _split_bf16(xn)
        xn_ref[...] = xn_hi
        fl_ref[...] = _dot3(xn_hi, xn_lo, wfh_ref[...], wfl_ref[...])

    acc = jnp.dot(xn_ref[...], w_ref[...], preferred_element_type=F32)
    col0 = j * tn
    is_q = (col0 >= q_ranges[0][0]) & (col0 < q_ranges[0][1])
    is_q = is_q | ((col0 >= q_ranges[1][0]) & (col0 < q_ranges[1][1]))
    scale = jnp.where(is_q, QK_SCALE * LOG2E, 1.0).astype(F32)

    @pl.when(j < n_rope)
    def _():
        cos = cos_ref[...] * scale
        sin = sin_ref[...] * scale
        for gi in range(tn // LANES):
            xg = acc[:, gi * LANES:(gi + 1) * LANES]
            rot = pltpu.roll(xg, HEAD_DIM // 2, 1)
            o_ref[:, gi * LANES:(gi + 1) * LANES] = (xg * cos + rot * sin).astype(BF16)

    @pl.when(j >= n_rope)
    def _():
        o_ref[...] = (acc * scale).astype(BF16)


def _inproj(x2, g, w_main, w_f_hi, w_f_lo, cos, sin_signed, cast_weights, *, diff_w, fox_w):
    n, d = x2.shape
    cols = w_main.shape[1]
    tm, tn = min(INPROJ_TM, n), INPROJ_TN
    n_rope = 2 * diff_w // tn
    q_ranges = ((0, diff_w), (3 * diff_w, 3 * diff_w + fox_w))
    cast_in, cast_out, cast_shapes = _cast_rows_specs(cast_weights, lambda i, j: i * (cols // tn) + j)
    body = functools.partial(_inproj_body, n_rope=n_rope, tn=tn, q_ranges=q_ranges,
                             cast_rows=tuple(w.shape[0] for w in cast_weights))
    return pl.pallas_call(
        body,
        out_shape=(jax.ShapeDtypeStruct((n, cols), BF16), jax.ShapeDtypeStruct((n, LANES), F32), *cast_shapes),
        grid=(n // tm, cols // tn),
        in_specs=[
            pl.BlockSpec((tm, d), lambda i, j: (i, 0)),
            pl.BlockSpec((1, d), lambda i, j: (0, 0)),
            pl.BlockSpec((d, tn), lambda i, j: (0, j)),
            pl.BlockSpec((d, LANES), lambda i, j: (0, 0)),
            pl.BlockSpec((d, LANES), lambda i, j: (0, 0)),
            pl.BlockSpec((tm, LANES), lambda i, j: (i, 0)),
            pl.BlockSpec((tm, LANES), lambda i, j: (i, 0)),
            *cast_in,
        ],
        out_specs=(pl.BlockSpec((tm, tn), lambda i, j: (i, j)),
                   pl.BlockSpec((tm, LANES), lambda i, j: (i, 0)), *cast_out),
        scratch_shapes=[pltpu.VMEM((tm, d), BF16)],
        compiler_params=_cparams(("arbitrary", "arbitrary"), BIG_VMEM_LIMIT),
        name="inproj",
    )(x2, g, w_main, w_f_hi, w_f_lo, cos, sin_signed, *cast_weights)


def _cumgate_body(z_ref, b_ref, c_ref):
    z = z_ref[...] + b_ref[...]
    v = jnp.minimum(z, 0.0) - jnp.log(1.0 + jnp.exp(-jnp.abs(z)))
    seq = v.shape[1]
    lane = lax.broadcasted_iota(jnp.int32, v.shape, 1)
    shift = 1
    while shift < seq:
        v = v + jnp.where(lane >= shift, pltpu.roll(v, shift, 1), 0.0)
        shift *= 2
    c_ref[...] = v


def _cumgate(z_rows, b_rows):
    return pl.pallas_call(
        _cumgate_body,
        out_shape=jax.ShapeDtypeStruct(z_rows.shape, F32),
        name="cumgate",
    )(z_rows, b_rows)


def _dot_nt(a, b):
    return lax.dot_general(a, b, (((1,), (1,)), ((), ())), preferred_element_type=F32)


def _build_vt(v_ref, vt_ref):
    seq, dv = v_ref.shape
    for c in range(seq // VT_CHUNK):
        blk = v_ref[c * VT_CHUNK:(c + 1) * VT_CHUNK, :].astype(F32)
        vt_ref[0:dv, c * VT_CHUNK:(c + 1) * VT_CHUNK] = blk.T.astype(BF16)
    vt_ref[dv:dv + ONES_ROWS, :] = jnp.ones((ONES_ROWS, seq), BF16)


def _softmax_step(s_t, vt_blk, m_ref, acc_ref):
    m_prev = m_ref[...]
    m_new = jnp.maximum(m_prev, jnp.max(s_t, axis=0, keepdims=True))
    alpha = jnp.exp2(m_prev - m_new)
    p_t = jnp.exp2(s_t - m_new).astype(BF16)
    acc_ref[...] = alpha * acc_ref[...] + jnp.dot(vt_blk, p_t, preferred_element_type=F32)
    m_ref[...] = m_new


def _sweep_keys(qi, t, step):
    def pair(i, carry):
        step(pl.multiple_of(i * (2 * t), 2 * t), 2 * t, False)
        return carry

    lax.fori_loop(0, qi // 2, pair, 0)

    @pl.when(qi % 2 == 1)
    def _():
        step(pl.multiple_of((qi - 1) * t, t), t, False)

    step(pl.multiple_of(qi * t, t), t, True)


def _normalised(acc, dv):
    return acc[0:dv] * (1.0 / acc[dv:dv + 1])


def _diff_body(lq1_ref, lk1_ref, lq2_ref, lk2_ref, gsub_ref, q1_ref, q2_ref, k1_ref, k2_ref, v_ref,
               *rest, t, lam_init, cast_rows):
    n_cast = len(cast_rows)
    cast_in, (o_ref, *cast_out), (vt, m1, a1, m2, a2) = rest[:n_cast], rest[n_cast:2 * n_cast + 1], rest[2 * n_cast + 1:]
    _cast_rows(_linear_step(3), cast_in, cast_out, cast_rows)
    qi = pl.program_id(2)
    dv = v_ref.shape[1]

    @pl.when(qi == 0)
    def _():
        _build_vt(v_ref, vt)

    for m_ref, a_ref in ((m1, a1), (m2, a2)):
        m_ref[...] = jnp.full(m_ref.shape, NEG_BIG, F32)
        a_ref[...] = jnp.zeros(a_ref.shape, F32)
    q1 = q1_ref[...]
    q2 = q2_ref[...]
    key_chunk = lax.broadcasted_iota(jnp.int32, (t, t), 0) // CHUNK
    qry_chunk = lax.broadcasted_iota(jnp.int32, (t, t), 1) // CHUNK
    visible = key_chunk <= qry_chunk

    def step(off, size, diagonal):
        vt_blk = vt[:, pl.ds(off, size)]
        for q, k_ref, m_ref, a_ref in ((q1, k1_ref, m1, a1), (q2, k2_ref, m2, a2)):
            s_t = _dot_nt(k_ref[pl.ds(off, size), :], q)
            if diagonal:
                s_t = jnp.where(visible, s_t, NEG_BIG)
            _softmax_step(s_t, vt_blk, m_ref, a_ref)

    _sweep_keys(qi, t, step)

    lam = (jnp.exp(jnp.sum(lq1_ref[...] * lk1_ref[...], axis=-1, keepdims=True))
           - jnp.exp(jnp.sum(lq2_ref[...] * lk2_ref[...], axis=-1, keepdims=True)) + lam_init)
    o = (_normalised(a1[...], dv) - lam * _normalised(a2[...], dv)).T
    ms = jnp.mean(o * o, axis=-1, keepdims=True)
    y = o * lax.rsqrt(ms + NORM_EPS) * gsub_ref[...]
    o_ref[...] = (y * (1.0 - lam_init)).astype(BF16)


def _diff_attention(proj, lq1, lk1, lq2, lk2, gsub, cast_weights, *, bsz, seq, diff_w, lam_init):
    t = min(DIFF_T, seq)
    dv = 2 * HEAD_DIM
    n_heads = diff_w // dv
    nq = seq // t
    kcol = diff_w // HEAD_DIM
    vcol = 2 * diff_w // dv
    vec = pl.BlockSpec((1, HEAD_DIM), lambda b, h, i: (0, 0))
    cast_in, cast_out, cast_shapes = _cast_rows_specs(cast_weights, lambda b, h, i: (b * n_heads + h) * nq + i)
    body = functools.partial(_diff_body, t=t, lam_init=lam_init, cast_rows=tuple(w.shape[0] for w in cast_weights))
    return pl.pallas_call(
        body,
        out_shape=(jax.ShapeDtypeStruct((bsz * seq, diff_w), BF16), *cast_shapes),
        grid=(bsz, n_heads, nq),
        in_specs=[
            vec, vec, vec, vec,
            pl.BlockSpec((1, dv), lambda b, h, i: (0, 0)),
            pl.BlockSpec((t, HEAD_DIM), lambda b, h, i: (b * nq + i, 2 * h)),
            pl.BlockSpec((t, HEAD_DIM), lambda b, h, i: (b * nq + i, 2 * h + 1)),
            pl.BlockSpec((seq, HEAD_DIM), lambda b, h, i: (b, kcol + 2 * h)),
            pl.BlockSpec((seq, HEAD_DIM), lambda b, h, i: (b, kcol + 2 * h + 1)),
            pl.BlockSpec((seq, dv), lambda b, h, i: (b, vcol + h)),
            *cast_in,
        ],
        out_specs=(pl.BlockSpec((t, dv), lambda b, h, i: (b * nq + i, h)), *cast_out),
        scratch_shapes=[pltpu.VMEM((dv + ONES_ROWS, seq), BF16),
                        pltpu.VMEM((1, t), F32), pltpu.VMEM((dv + ONES_ROWS, t), F32),
                        pltpu.VMEM((1, t), F32), pltpu.VMEM((dv + ONES_ROWS, t), F32)],
        compiler_params=_cparams(("arbitrary", "arbitrary", "arbitrary")),
        name="diffattn",
    )(lq1, lk1, lq2, lk2, gsub, proj, proj, proj, proj, proj, *cast_weights)


def _fox_body(g_ref, q_ref, k_ref, ck_ref, v_ref, *rest, t, cast_rows):
    n_cast = len(cast_rows)
    cast_in, (o_ref, *cast_out), (vt, ck_lanes, m, acc) = rest[:n_cast], rest[n_cast:2 * n_cast + 1], rest[2 * n_cast + 1:]
    _cast_rows(_linear_step(3), cast_in, cast_out, cast_rows)
    qi = pl.program_id(2)
    dv = v_ref.shape[1]

    @pl.when(qi == 0)
    def _():
        _build_vt(v_ref, vt)
        ck_lanes[...] = jnp.broadcast_to(ck_ref[...], ck_lanes.shape)

    m[...] = jnp.full(m.shape, NEG_BIG, F32)
    acc[...] = jnp.zeros(acc.shape, F32)
    q = q_ref[...]
    causal = lax.broadcasted_iota(jnp.int32, (t, t), 0) <= lax.broadcasted_iota(jnp.int32, (t, t), 1)

    def step(off, size, diagonal):
        ck = ck_lanes[pl.ds(off, size), :]
        s_t = _dot_nt(k_ref[pl.ds(off, size), :], q) - jnp.concatenate([ck] * (t // LANES), axis=1)
        if diagonal:
            s_t = jnp.where(causal, s_t, NEG_BIG)
        _softmax_step(s_t, vt[:, pl.ds(off, size)], m, acc)

    _sweep_keys(qi, t, step)

    o = _normalised(acc[...], dv).T
    ms = jnp.mean(o * o, axis=-1, keepdims=True)
    o_ref[...] = (o * lax.rsqrt(ms + NORM_EPS) * g_ref[...]).astype(BF16)


def _fox_attention(proj, c_cols, g_fox, cast_weights, *, bsz, seq, diff_w, fox_w):
    t = min(FOX_T, seq)
    n_heads = fox_w // HEAD_DIM
    nq = seq // t
    qcol = 3 * diff_w // HEAD_DIM
    kcol = qcol + n_heads
    vcol = kcol + n_heads
    cast_in, cast_out, cast_shapes = _cast_rows_specs(cast_weights, lambda b, h, i: (b * n_heads + h) * nq + i)
    body = functools.partial(_fox_body, t=t, cast_rows=tuple(w.shape[0] for w in cast_weights))
    return pl.pallas_call(
        body,
        out_shape=(jax.ShapeDtypeStruct((bsz * seq, fox_w), BF16), *cast_shapes),
        grid=(bsz, n_heads, nq),
        in_specs=[
            pl.BlockSpec((1, HEAD_DIM), lambda b, h, i: (0, 0)),
            pl.BlockSpec((t, HEAD_DIM), lambda b, h, i: (b * nq + i, qcol + h)),
            pl.BlockSpec((seq, HEAD_DIM), lambda b, h, i: (b, kcol + h)),
            pl.BlockSpec((None, seq, 1), lambda b, h, i: (b * n_heads + h, 0, 0)),
            pl.BlockSpec((seq, HEAD_DIM), lambda b, h, i: (b, vcol + h)),
            *cast_in,
        ],
        out_specs=(pl.BlockSpec((t, HEAD_DIM), lambda b, h, i: (b * nq + i, h)), *cast_out),
        scratch_shapes=[pltpu.VMEM((HEAD_DIM + ONES_ROWS, seq), BF16), pltpu.VMEM((seq, LANES), F32),
                        pltpu.VMEM((1, t), F32), pltpu.VMEM((HEAD_DIM + ONES_ROWS, t), F32)],
        compiler_params=_cparams(("arbitrary", "arbitrary", "arbitrary"), BIG_VMEM_LIMIT),
        name="foxattn",
    )(g_fox, proj, proj, c_cols, proj, *cast_weights)


def _outproj_body(od_ref, of_ref, wd_ref, wf_ref, x_ref, g_ref, wrh_ref, wrl_ref, br_ref, h_ref, hn_ref, lg_ref):
    h = (x_ref[...]
         + jnp.dot(od_ref[...], wd_ref[...], preferred_element_type=F32)
         + jnp.dot(of_ref[...], wf_ref[...], preferred_element_type=F32))
    h_ref[...] = h
    ms = jnp.mean(h * h, axis=-1, keepdims=True)
    hn = h * lax.rsqrt(ms + NORM_EPS) * g_ref[...]
    hn_ref[...] = hn
    hn_hi, hn_lo = _split_bf16(hn)
    lg_ref[...] = _dot3(hn_hi, hn_lo, wrh_ref[...], wrl_ref[...]) + br_ref[...]


def _outproj(od, of, w_od, w_of, x2, g_ffn, w_router_hi, w_router_lo, b_router_p):
    n, d = x2.shape
    tm = OUTPROJ_TM
    row = lambda i: (i, 0)
    const = lambda i: (0, 0)
    resident = functools.partial(pl.BlockSpec, index_map=const, pipeline_mode=pl.Buffered(1))
    return pl.pallas_call(
        _outproj_body,
        out_shape=(jax.ShapeDtypeStruct((n, d), F32), jax.ShapeDtypeStruct((n, d), F32),
                   jax.ShapeDtypeStruct((n, LANES), F32)),
        grid=(n // tm,),
        in_specs=[
            pl.BlockSpec((tm, od.shape[1]), row),
            pl.BlockSpec((tm, of.shape[1]), row),
            resident(w_od.shape),
            resident(w_of.shape),
            pl.BlockSpec((tm, d), row),
            pl.BlockSpec((1, d), const),
            resident((d, LANES)),
            resident((d, LANES)),
            pl.BlockSpec((1, LANES), const),
        ],
        out_specs=(pl.BlockSpec((tm, d), row), pl.BlockSpec((tm, d), row), pl.BlockSpec((tm, LANES), row)),
        compiler_params=_cparams(("parallel",)),
        name="outproj",
    )(od, of, w_od, w_of, x2, g_ffn, w_router_hi, w_router_lo, b_router_p)


def _route_body(lg_ref, e_ref, gate_ref, rank_ref, cnt_ref, carry_ref, *, n_experts):
    i = pl.program_id(0)
    tm = lg_ref.shape[0]

    @pl.when(i == 0)
    def _():
        carry_ref[...] = jnp.zeros(carry_ref.shape, F32)

    lane = lax.broadcasted_iota(jnp.int32, (tm, LANES), 1)
    logits = jnp.where(lane < n_experts, lg_ref[...], -jnp.inf)
    vals, idxs, hots = [], [], []
    for _ in range(TOP_K):
        mx = jnp.max(logits, axis=-1, keepdims=True)
        idx = jnp.min(jnp.where(logits == mx, lane, LANES), axis=-1, keepdims=True)
        hot = lane == idx
        logits = jnp.where(hot, -jnp.inf, logits)
        vals.append(mx)
        idxs.append(idx)
        hots.append(hot)

    exps = [jnp.exp(v - vals[0]) for v in vals]
    denom = exps[0]
    for ex in exps[1:]:
        denom = denom + ex
    inv = 1.0 / denom

    hot_sum = jnp.zeros((tm, LANES), F32)
    for hot in hots:
        hot_sum = hot_sum + jnp.where(hot, 1.0, 0.0)
    r = lax.broadcasted_iota(jnp.int32, (tm, tm), 0)
    c = lax.broadcasted_iota(jnp.int32, (tm, tm), 1)
    earlier = jnp.where(c < r, 1.0, 0.0).astype(BF16)
    base = carry_ref[...] + jnp.dot(earlier, hot_sum.astype(BF16), preferred_element_type=F32)

    e_out = jnp.zeros((tm, LANES), jnp.int32)
    g_out = jnp.zeros((tm, LANES), F32)
    r_out = jnp.zeros((tm, LANES), jnp.int32)
    for k in range(TOP_K):
        rank = jnp.sum(jnp.where(hots[k], base, 0.0), axis=-1, keepdims=True).astype(jnp.int32)
        e_out = jnp.where(lane == k, idxs[k], e_out)
        g_out = jnp.where(lane == k, exps[k] * inv, g_out)
        r_out = jnp.where(lane == k, rank, r_out)
    e_ref[...] = e_out
    gate_ref[...] = g_out
    rank_ref[...] = r_out
    carry_ref[...] = carry_ref[...] + jnp.sum(hot_sum, axis=0, keepdims=True)
    cnt_ref[...] = carry_ref[...]


def _route(logits, n_experts):
    n = logits.shape[0]
    tm = ROUTE_TM
    row = lambda i: (i, 0)
    body = functools.partial(_route_body, n_experts=n_experts)
    return pl.pallas_call(
        body,
        out_shape=(jax.ShapeDtypeStruct((n, LANES), jnp.int32), jax.ShapeDtypeStruct((n, LANES), F32),
                   jax.ShapeDtypeStruct((n, LANES), jnp.int32), jax.ShapeDtypeStruct((1, LANES), F32)),
        grid=(n // tm,),
        in_specs=[pl.BlockSpec((tm, LANES), row)],
        out_specs=(pl.BlockSpec((tm, LANES), row), pl.BlockSpec((tm, LANES), row),
                   pl.BlockSpec((tm, LANES), row), pl.BlockSpec((1, LANES), lambda i: (0, 0))),
        scratch_shapes=[pltpu.VMEM((1, LANES), F32)],
        compiler_params=_cparams(("arbitrary",)),
        name="route",
    )(logits)


def _dispatch_body(ztile_ref, dest_ref, hn_ref, xs_ref, zeros, sem, zsem):
    tm = hn_ref.shape[0]

    @pl.when(pl.program_id(0) == 0)
    def _():
        zeros[...] = jnp.zeros(zeros.shape, zeros.dtype)

        def fill(z):
            row0 = pl.multiple_of(ztile_ref[z] * EXPERT_TM, EXPERT_TM)
            return pltpu.make_async_copy(zeros, xs_ref.at[pl.ds(row0, EXPERT_TM), :], zsem)

        def fill_start(z, carry):
            @pl.when(ztile_ref[z] >= 0)
            def _():
                fill(z).start()
            return carry

        def fill_wait(z, carry):
            @pl.when(ztile_ref[z] >= 0)
            def _():
                fill(z).wait()
            return carry

        lax.fori_loop(0, ztile_ref.shape[0], fill_start, 0)
        lax.fori_loop(0, ztile_ref.shape[0], fill_wait, 0)

    def start(t, carry):
        src = hn_ref.at[pl.ds(t, 1), :]
        for k in range(TOP_K):
            pltpu.make_async_copy(src, xs_ref.at[pl.ds(dest_ref[0, t * TOP_K + k], 1), :], sem).start()
        return carry

    lax.fori_loop(0, tm, start, 0)
    for _ in range(TOP_K):
        pltpu.make_async_copy(hn_ref, xs_ref.at[pl.ds(0, tm), :], sem).wait()


def _dispatch(zero_tiles, dest, hn, n_rows):
    n, d = hn.shape
    tm = DISPATCH_TM
    dest3 = dest.reshape(n // tm, 1, tm * TOP_K)
    return pl.pallas_call(
        _dispatch_body,
        out_shape=jax.ShapeDtypeStruct((n_rows, d), hn.dtype),
        grid_spec=pltpu.PrefetchScalarGridSpec(
            num_scalar_prefetch=1,
            grid=(n // tm,),
            in_specs=[
                pl.BlockSpec((None, 1, tm * TOP_K), lambda i, zt: (i, 0, 0), memory_space=pltpu.SMEM),
                pl.BlockSpec((tm, d), lambda i, zt: (i, 0)),
            ],
            out_specs=pl.BlockSpec(memory_space=pl.ANY),
            scratch_shapes=[pltpu.VMEM((EXPERT_TM, d), hn.dtype), pltpu.SemaphoreType.DMA,
                            pltpu.SemaphoreType.DMA],
        ),
        compiler_params=_cparams(("arbitrary",)),
        name="dispatch",
    )(zero_tiles, dest3, hn)


def _experts_body(te_ref, nu_ref, live_ref, x_ref, wg_ref, bg_ref, wu_ref, bu_ref, wd_ref, bd_ref, o_ref, xb_ref):
    i = pl.program_id(0)
    j = pl.program_id(1)
    tm = x_ref.shape[0]

    def compute(r):
        @pl.when(j == 0)
        def _():
            xb_ref[0:r, :] = x_ref[0:r, :].astype(BF16)
            o_ref[0:r, :] = jnp.broadcast_to(bd_ref[...], (r, o_ref.shape[1]))
            if r < tm:
                o_ref[r:tm, :] = jnp.zeros((tm - r, o_ref.shape[1]), F32)

        xb = xb_ref[0:r, :]
        g = jnp.dot(xb, wg_ref[...], preferred_element_type=F32) + bg_ref[...]
        u = jnp.dot(xb, wu_ref[...], preferred_element_type=F32) + bu_ref[...]
        g = jnp.minimum(g, SWIGLU_LIMIT)
        u = jnp.clip(u, -SWIGLU_LIMIT, SWIGLU_LIMIT)
        hdn = g * (1.0 / (1.0 + jnp.exp(-SWIGLU_ALPHA * g))) * (u + 1.0)
        o_ref[0:r, :] += jnp.dot(hdn.astype(BF16), wd_ref[...], preferred_element_type=F32)

    for r in range(EXPERT_RT, tm + 1, EXPERT_RT):
        pl.when(live_ref[i] == r)(functools.partial(compute, r))

    @pl.when((live_ref[i] == 0) & (j == 0))
    def _():
        o_ref[...] = jnp.zeros(o_ref.shape, F32)


def _experts(tile_expert, n_used, tile_live, xs, wg, bg, wu, bu, wd, bd):
    rows, d = xs.shape
    d_exp = wg.shape[2]
    tm, tc = EXPERT_TM, EXPERT_TC
    n_tiles = rows // tm
    nj = d_exp // tc

    def tile(i, nu):
        return jnp.minimum(i, nu[0] - 1)

    def chunk(i, j, nu):
        return jnp.where(i < nu[0], j, nj - 1)

    return pl.pallas_call(
        _experts_body,
        out_shape=jax.ShapeDtypeStruct((rows, d), F32),
        grid_spec=pltpu.PrefetchScalarGridSpec(
            num_scalar_prefetch=3,
            grid=(n_tiles, nj),
            in_specs=[
                pl.BlockSpec((tm, d), lambda i, j, te, nu, live: (tile(i, nu), 0)),
                pl.BlockSpec((None, d, tc), lambda i, j, te, nu, live: (te[i], 0, chunk(i, j, nu))),
                pl.BlockSpec((None, 1, tc), lambda i, j, te, nu, live: (te[i], 0, chunk(i, j, nu))),
                pl.BlockSpec((None, d, tc), lambda i, j, te, nu, live: (te[i], 0, chunk(i, j, nu))),
                pl.BlockSpec((None, 1, tc), lambda i, j, te, nu, live: (te[i], 0, chunk(i, j, nu))),
                pl.BlockSpec((None, tc, d), lambda i, j, te, nu, live: (te[i], chunk(i, j, nu), 0)),
                pl.BlockSpec((None, 1, d), lambda i, j, te, nu, live: (te[i], 0, 0)),
            ],
            out_specs=pl.BlockSpec((tm, d), lambda i, j, te, nu, live: (i, 0)),
            scratch_shapes=[pltpu.VMEM((tm, d), BF16)],
        ),
        compiler_params=_cparams(("arbitrary", "arbitrary")),
        name="experts",
    )(tile_expert, n_used, tile_live, xs, wg, bg, wu, bu, wd, bd)


def _combine_body(dcur_ref, dnext_ref, gate_ref, h_ref, gfin_ref, ys_ref, o_ref, buf, sems, *, final_norm):
    i = pl.program_id(0)
    nb = pl.num_programs(0)
    tm = h_ref.shape[0]
    slot = i % 2

    def start_all(dref, s):
        def go(t, carry):
            for k in range(TOP_K):
                pltpu.make_async_copy(ys_ref.at[pl.ds(dref[0, t * TOP_K + k], 1), :],
                                      buf.at[s, k, pl.ds(t, 1), :], sems.at[s]).start()
            return carry
        lax.fori_loop(0, tm, go, 0)

    @pl.when(i == 0)
    def _():
        start_all(dcur_ref, 0)

    @pl.when(i + 1 < nb)
    def _():
        start_all(dnext_ref, 1 - slot)

    for k in range(TOP_K):
        pltpu.make_async_copy(ys_ref.at[pl.ds(0, tm), :], buf.at[slot, k], sems.at[slot]).wait()

    y = h_ref[...]
    gates = gate_ref[...]
    for k in range(TOP_K):
        y = y + gates[:, k:k + 1] * buf[slot, k]
    if final_norm:
        ms = jnp.mean(y * y, axis=-1, keepdims=True)
        y = y * lax.rsqrt(ms + NORM_EPS) * gfin_ref[...]
    o_ref[...] = y


def _combine(dest, gates, h, g_final, ys, *, final_norm):
    n, d = h.shape
    tm = COMBINE_TM
    nb = n // tm
    dest3 = dest.reshape(nb, 1, tm * TOP_K)
    smem_blk = (None, 1, tm * TOP_K)
    return pl.pallas_call(
        functools.partial(_combine_body, final_norm=final_norm),
        out_shape=jax.ShapeDtypeStruct((n, d), F32),
        grid=(nb,),
        in_specs=[
            pl.BlockSpec(smem_blk, lambda i: (i, 0, 0), memory_space=pltpu.SMEM),
            pl.BlockSpec(smem_blk, lambda i: (jnp.minimum(i + 1, nb - 1), 0, 0), memory_space=pltpu.SMEM),
            pl.BlockSpec((tm, LANES), lambda i: (i, 0)),
            pl.BlockSpec((tm, d), lambda i: (i, 0)),
            pl.BlockSpec((1, d), lambda i: (0, 0)),
            pl.BlockSpec(memory_space=pl.ANY),
        ],
        out_specs=pl.BlockSpec((tm, d), lambda i: (i, 0)),
        scratch_shapes=[pltpu.VMEM((2, TOP_K, tm, d), F32), pltpu.SemaphoreType.DMA((2,))],
        compiler_params=_cparams(("arbitrary",)),
        name="combine",
    )(dest3, dest3, gates, h, g_final, ys)


def _rope_tables(positions):
    inv_freq = ROPE_THETA ** (-jnp.arange(0, HEAD_DIM, 2, dtype=F32) / HEAD_DIM)
    ang = positions.astype(F32)[..., None] * inv_freq
    ang = jnp.concatenate([ang, ang], axis=-1)
    sign = jnp.concatenate([-jnp.ones((HEAD_DIM // 2,), F32), jnp.ones((HEAD_DIM // 2,), F32)])
    return jnp.cos(ang), jnp.sin(ang) * sign


def _layer(x2, cos, sin_signed, layer_idx, bsz, seq, g_attn_norm, w_in, b_forget, lambda_q1, lambda_k1,
           lambda_q2, lambda_k2, g_diff_sub, g_fox_out, w_out, g_ffn_norm, w_router, b_router,
           w_gate, b_gate, w_up, b_up, w_down, b_down):
    n, d = x2.shape
    diff_w = d // 2
    fox_w = d - diff_w
    fox_heads = fox_w // HEAD_DIM
    main_cols = 3 * diff_w + 3 * fox_w
    n_experts = w_router.shape[1]
    lam_init = _lambda_init(layer_idx)

    d_exp = w_gate.shape[2]
    hosted = {"inproj": w_up.reshape(n_experts * d, d_exp),
              "diff": w_down.reshape(n_experts * d_exp, d),
              "fox": w_gate.reshape(n_experts * d, d_exp)}
    host_steps = {"inproj": (n // min(INPROJ_TM, n)) * (main_cols // INPROJ_TN),
                  "diff": bsz * (diff_w // (2 * HEAD_DIM)) * (seq // min(DIFF_T, seq)),
                  "fox": bsz * fox_heads * (seq // min(FOX_T, seq))}
    fused = {k: _can_cast_rows((w,), host_steps[k]) for k, w in hosted.items()}
    cast_arg = {k: (w,) if fused[k] else () for k, w in hosted.items()}

    w_main = w_in[:, :main_cols].astype(BF16)
    w_f_hi, w_f_lo = _split_bf16(jnp.pad(w_in[:, main_cols:], ((0, 0), (0, LANES - fox_heads))))
    proj, f_logit, *up_b = _inproj(x2, g_attn_norm[None, :], w_main, w_f_hi, w_f_lo, cos, sin_signed,
                                   cast_arg["inproj"], diff_w=diff_w, fox_w=fox_w)

    z_rows = f_logit[:, :fox_heads].reshape(bsz, seq, fox_heads).transpose(0, 2, 1).reshape(bsz * fox_heads, seq)
    b_rows = jnp.tile(b_forget.astype(F32), bsz)[:, None]
    c_rows = _cumgate(z_rows, b_rows)

    od, *down_b = _diff_attention(proj, lambda_q1[None, :], lambda_k1[None, :], lambda_q2[None, :],
                                  lambda_k2[None, :], g_diff_sub[None, :], cast_arg["diff"],
                                  bsz=bsz, seq=seq, diff_w=diff_w, lam_init=lam_init)
    of, *gate_b = _fox_attention(proj, (c_rows * LOG2E)[:, :, None], g_fox_out[None, :], cast_arg["fox"],
                                 bsz=bsz, seq=seq, diff_w=diff_w, fox_w=fox_w)
    w_up_b = (up_b[0] if fused["inproj"] else hosted["inproj"].astype(BF16)).reshape(w_up.shape)
    w_down_b = (down_b[0] if fused["diff"] else hosted["diff"].astype(BF16)).reshape(w_down.shape)
    w_gate_b = (gate_b[0] if fused["fox"] else hosted["fox"].astype(BF16)).reshape(w_gate.shape)

    w_out_b = w_out.astype(BF16)
    w_router_hi, w_router_lo = _split_bf16(jnp.pad(w_router, ((0, 0), (0, LANES - n_experts))))
    b_router_p = jnp.pad(b_router, (0, LANES - n_experts))[None, :]
    h, hn, logits = _outproj(od, of, w_out_b[:diff_w], w_out_b[diff_w:], x2, g_ffn_norm[None, :],
                             w_router_hi, w_router_lo, b_router_p)

    e_pad, gate_pad, rank_pad, cnt = _route(logits, n_experts)
    e_idx = e_pad[:, :TOP_K]
    counts = cnt[0, :n_experts].astype(jnp.int32)
    padded = (counts + EXPERT_TM - 1) // EXPERT_TM * EXPERT_TM
    pad_ends = jnp.cumsum(padded)
    pad_starts = pad_ends - padded
    start_of = jnp.sum(jnp.where(e_idx[..., None] == jnp.arange(n_experts), pad_starts, 0), axis=-1)
    dest = (start_of + rank_pad[:, :TOP_K]).astype(jnp.int32)
    n_tiles = -(-n * TOP_K // EXPERT_TM) + n_experts
    tile_start = jnp.arange(n_tiles, dtype=jnp.int32) * EXPERT_TM
    tile_expert = jnp.sum((pad_ends[None, :] <= tile_start[:, None]).astype(jnp.int32), axis=1)
    tile_expert = jnp.minimum(tile_expert, n_experts - 1)
    n_used = (pad_ends[-1:] // EXPERT_TM).astype(jnp.int32)
    tile_expert = jnp.where(jnp.arange(n_tiles) < n_used[0], tile_expert, tile_expert[n_used[0] - 1])
    last_tile = jnp.where(padded > 0, pad_ends // EXPERT_TM - 1, -1)
    tail_tile = n_used[0] + jnp.arange(n_tiles - n * TOP_K // EXPERT_TM, dtype=jnp.int32)
    tail_tile = jnp.where(tail_tile < n_tiles, tail_tile, -1)
    zero_tiles = jnp.concatenate([last_tile, tail_tile]).astype(jnp.int32)

    routed_end = (pad_starts + counts)[tile_expert]
    tile_live = jnp.clip(routed_end - tile_start, 0, EXPERT_TM)
    tile_live = jnp.where(jnp.arange(n_tiles) < n_used[0], tile_live, 0)
    tile_live = ((tile_live + EXPERT_RT - 1) // EXPERT_RT * EXPERT_RT).astype(jnp.int32)

    xs = _dispatch(zero_tiles, dest, hn, n_tiles * EXPERT_TM)
    ys = _experts(tile_expert, n_used, tile_live, xs, w_gate_b, b_gate[:, None, :], w_up_b, b_up[:, None, :],
                  w_down_b, b_down[:, None, :])
    return dest, gate_pad, h, ys


def kernel(x, positions, g_attn_norm, w_in, b_forget, lambda_q1, lambda_k1, lambda_q2, lambda_k2, g_diff_sub,
           g_fox_out, w_out, g_ffn_norm, w_router, b_router, w_gate, b_gate, w_up, b_up, w_down, b_down,
           g_final):
    bsz, seq, d = x.shape
    depth = w_in.shape[0]
    cos, sin_signed = _rope_tables(positions)
    cos = cos.reshape(bsz * seq, HEAD_DIM)
    sin_signed = sin_signed.reshape(bsz * seq, HEAD_DIM)
    x2 = x.reshape(bsz * seq, d)
    for l in range(depth):
        dest, gates, h, ys = _layer(
            x2, cos, sin_signed, l, bsz, seq, g_attn_norm[l], w_in[l], b_forget[l], lambda_q1[l], lambda_k1[l],
            lambda_q2[l], lambda_k2[l], g_diff_sub[l], g_fox_out[l], w_out[l], g_ffn_norm[l], w_router[l],
            b_router[l], w_gate[l], b_gate[l], w_up[l], b_up[l], w_down[l], b_down[l])
        x2 = _combine(dest, gates, h, g_final[None, :], ys, final_norm=(l == depth - 1))
    return x2.reshape(bsz, seq, d)
```

```python
import functools
import math

import jax
import jax.numpy as jnp
from jax import lax
from jax.experimental import pallas as pl
from jax.experimental.pallas import tpu as pltpu

F32 = jnp.float32
BF16 = jnp.bfloat16

HEAD_DIM = 128
CHUNK = 64
ROPE_THETA = 10000.0
NORM_EPS = 1e-5
NEG_BIG = -1e30
TOP_K = 4
SWIGLU_LIMIT = 7.0
SWIGLU_ALPHA = 1.702
LANES = 128
QK_SCALE = HEAD_DIM ** -0.5
LOG2E = math.log2(math.e)
ONES_ROWS = 16

INPROJ_TM = 1024
INPROJ_TN = 512
DIFF_T = 512
FOX_T = 1024
CAST_ROWS = 512
VT_CHUNK = 512
OUTPROJ_TM = 512
ROUTE_TM = 512
DISPATCH_TM = 256
EXPERT_TM = 512
EXPERT_RT = 128
EXPERT_TC = 1024
COMBINE_TM = 128
VMEM_LIMIT = 48 * 1024 * 1024
BIG_VMEM_LIMIT = 58 * 1024 * 1024


def _lambda_init(layer_idx):
    return 0.8 - 0.6 * math.exp(-0.3 * layer_idx)


def _cparams(sem, vmem_limit=VMEM_LIMIT):
    return pltpu.CompilerParams(dimension_semantics=sem, vmem_limit_bytes=vmem_limit)


def _split_bf16(x):
    hi = x.astype(BF16)
    return hi, (x - hi.astype(F32)).astype(BF16)


def _dot3(x_hi, x_lo, w_hi, w_lo):
    dot = functools.partial(jnp.dot, preferred_element_type=F32)
    return dot(x_hi, w_hi) + (dot(x_hi, w_lo) + dot(x_lo, w_hi))


def _cast_rows_specs(weights, step_of):
    in_specs, out_specs, out_shapes = [], [], []
    for w in weights:
        rows, cols = w.shape
        last = rows // CAST_ROWS - 1
        spec = pl.BlockSpec((CAST_ROWS, cols), lambda *idx, last=last: (jnp.minimum(step_of(*idx), last), 0))
        in_specs.append(spec)
        out_specs.append(spec)
        out_shapes.append(jax.ShapeDtypeStruct((rows, cols), BF16))
    return in_specs, out_specs, out_shapes


def _can_cast_rows(weights, steps):
    return all(w.shape[0] % CAST_ROWS == 0 and w.shape[0] // CAST_ROWS <= steps for w in weights)


def _linear_step(n_axes):
    step = pl.program_id(0)
    for axis in range(1, n_axes):
        step = step * pl.num_programs(axis) + pl.program_id(axis)
    return step


def _cast_rows(step, in_refs, out_refs, total_rows):
    for src, dst, rows in zip(in_refs, out_refs, total_rows):
        @pl.when(step < rows // CAST_ROWS)
        def _():
            dst[...] = src[...].astype(BF16)


def _inproj_body(x_ref, g_ref, w_ref, wfh_ref, wfl_ref, cos_ref, sin_ref, *rest, n_rope, tn, q_ranges, cast_rows):
    n_cast = len(cast_rows)
    cast_in, (o_ref, fl_ref, *cast_out), (xn_ref,) = rest[:n_cast], rest[n_cast:2 * n_cast + 2], rest[2 * n_cast + 2:]
    _cast_rows(_linear_step(2), cast_in, cast_out, cast_rows)
    j = pl.program_id(1)

    @pl.when(j == 0)
    def _():
        x = x_ref[...]
        ms = jnp.mean(x * x, axis=-1, keepdims=True)
        xn = x * lax.rsqrt(ms + NORM_EPS) * g_ref[...]
        xn_hi, xn_lo = _split_bf16(xn)
        xn_ref[...] = xn_hi
        fl_ref[...] = _dot3(xn_hi, xn_lo, wfh_ref[...], wfl_ref[...])

    acc = jnp.dot(xn_ref[...], w_ref[...], preferred_element_type=F32)
    col0 = j * tn
    is_q = (col0 >= q_ranges[0][0]) & (col0 < q_ranges[0][1])
    is_q = is_q | ((col0 >= q_ranges[1][0]) & (col0 < q_ranges[1][1]))
    scale = jnp.where(is_q, QK_SCALE * LOG2E, 1.0).astype(F32)

    @pl.when(j < n_rope)
    def _():
        cos = cos_ref[...] * scale
        sin = sin_ref[...] * scale
        for gi in range(tn // LANES):
            xg = acc[:, gi * LANES:(gi + 1) * LANES]
            rot = pltpu.roll(xg, HEAD_DIM // 2, 1)
            o_ref[:, gi * LANES:(gi + 1) * LANES] = (xg * cos + rot * sin).astype(BF16)

    @pl.when(j >= n_rope)
    def _():
        o_ref[...] = (acc * scale).astype(BF16)


def _inproj(x2, g, w_main, w_f_hi, w_f_lo, cos, sin_signed, cast_weights, *, diff_w, fox_w):
    n, d = x2.shape
    cols = w_main.shape[1]
    tm, tn = min(INPROJ_TM, n), INPROJ_TN
    n_rope = 2 * diff_w // tn
    q_ranges = ((0, diff_w), (3 * diff_w, 3 * diff_w + fox_w))
    cast_in, cast_out, cast_shapes = _cast_rows_specs(cast_weights, lambda i, j: i * (cols // tn) + j)
    body = functools.partial(_inproj_body, n_rope=n_rope, tn=tn, q_ranges=q_ranges,
                             cast_rows=tuple(w.shape[0] for w in cast_weights))
    return pl.pallas_call(
        body,
        out_shape=(jax.ShapeDtypeStruct((n, cols), BF16), jax.ShapeDtypeStruct((n, LANES), F32), *cast_shapes),
        grid=(n // tm, cols // tn),
        in_specs=[
            pl.BlockSpec((tm, d), lambda i, j: (i, 0)),
            pl.BlockSpec((1, d), lambda i, j: (0, 0)),
            pl.BlockSpec((d, tn), lambda i, j: (0, j)),
            pl.BlockSpec((d, LANES), lambda i, j: (0, 0)),
            pl.BlockSpec((d, LANES), lambda i, j: (0, 0)),
            pl.BlockSpec((tm, LANES), lambda i, j: (i, 0)),
            pl.BlockSpec((tm, LANES), lambda i, j: (i, 0)),
            *cast_in,
        ],
        out_specs=(pl.BlockSpec((tm, tn), lambda i, j: (i, j)),
                   pl.BlockSpec((tm, LANES), lambda i, j: (i, 0)), *cast_out),
        scratch_shapes=[pltpu.VMEM((tm, d), BF16)],
        compiler_params=_cparams(("arbitrary", "arbitrary"), BIG_VMEM_LIMIT),
        name="inproj",
    )(x2, g, w_main, w_f_hi, w_f_lo, cos, sin_signed, *cast_weights)


def _cumgate_body(z_ref, b_ref, c_ref):
    z = z_ref[...] + b_ref[...]
    v = jnp.minimum(z, 0.0) - jnp.log(1.0 + jnp.exp(-jnp.abs(z)))
    seq = v.shape[1]
    lane = lax.broadcasted_iota(jnp.int32, v.shape, 1)
    shift = 1
    while shift < seq:
        v = v + jnp.where(lane >= shift, pltpu.roll(v, shift, 1), 0.0)
        shift *= 2
    c_ref[...] = v


def _cumgate(z_rows, b_rows):
    return pl.pallas_call(
        _cumgate_body,
        out_shape=jax.ShapeDtypeStruct(z_rows.shape, F32),
        name="cumgate",
    )(z_rows, b_rows)


def _dot_nt(a, b):
    return lax.dot_general(a, b, (((1,), (1,)), ((), ())), preferred_element_type=F32)


def _build_vt(v_ref, vt_ref):
    seq, dv = v_ref.shape
    for c in range(seq // VT_CHUNK):
        blk = v_ref[c * VT_CHUNK:(c + 1) * VT_CHUNK, :].astype(F32)
        vt_ref[0:dv, c * VT_CHUNK:(c + 1) * VT_CHUNK] = blk.T.astype(BF16)
    vt_ref[dv:dv + ONES_ROWS, :] = jnp.ones((ONES_ROWS, seq), BF16)


def _softmax_step(s_t, vt_blk, m_ref, acc_ref):
    m_prev = m_ref[...]
    m_new = jnp.maximum(m_prev, jnp.max(s_t, axis=0, keepdims=True))
    alpha = jnp.exp2(m_prev - m_new)
    p_t = jnp.exp2(s_t - m_new).astype(BF16)
    acc_ref[...] = alpha * acc_ref[...] + jnp.dot(vt_blk, p_t, preferred_element_type=F32)
    m_ref[...] = m_new


def _sweep_keys(qi, t, step):
    def pair(i, carry):
        step(pl.multiple_of(i * (2 * t), 2 * t), 2 * t, False)
        return carry

    lax.fori_loop(0, qi // 2, pair, 0)

    @pl.when(qi % 2 == 1)
    def _():
        step(pl.multiple_of((qi - 1) * t, t), t, False)

    step(pl.multiple_of(qi * t, t), t, True)


def _normalised(acc, dv):
    return acc[0:dv] * (1.0 / acc[dv:dv + 1])


def _diff_body(lq1_ref, lk1_ref, lq2_ref, lk2_ref, gsub_ref, q1_ref, q2_ref, k1_ref, k2_ref, v_ref,
               *rest, t, lam_init, cast_rows):
    n_cast = len(cast_rows)
    cast_in, (o_ref, *cast_out), (vt, m1, a1, m2, a2) = rest[:n_cast], rest[n_cast:2 * n_cast + 1], rest[2 * n_cast + 1:]
    _cast_rows(_linear_step(3), cast_in, cast_out, cast_rows)
    qi = pl.program_id(2)
    dv = v_ref.shape[1]

    @pl.when(qi == 0)
    def _():
        _build_vt(v_ref, vt)

    for m_ref, a_ref in ((m1, a1), (m2, a2)):
        m_ref[...] = jnp.full(m_ref.shape, NEG_BIG, F32)
        a_ref[...] = jnp.zeros(a_ref.shape, F32)
    q1 = q1_ref[...]
    q2 = q2_ref[...]
    key_chunk = lax.broadcasted_iota(jnp.int32, (t, t), 0) // CHUNK
    qry_chunk = lax.broadcasted_iota(jnp.int32, (t, t), 1) // CHUNK
    visible = key_chunk <= qry_chunk

    def step(off, size, diagonal):
        vt_blk = vt[:, pl.ds(off, size)]
        for q, k_ref, m_ref, a_ref in ((q1, k1_ref, m1, a1), (q2, k2_ref, m2, a2)):
            s_t = _dot_nt(k_ref[pl.ds(off, size), :], q)
            if diagonal:
                s_t = jnp.where(visible, s_t, NEG_BIG)
            _softmax_step(s_t, vt_blk, m_ref, a_ref)

    _sweep_keys(qi, t, step)

    lam = (jnp.exp(jnp.sum(lq1_ref[...] * lk1_ref[...], axis=-1, keepdims=True))
           - jnp.exp(jnp.sum(lq2_ref[...] * lk2_ref[...], axis=-1, keepdims=True)) + lam_init)
    o = (_normalised(a1[...], dv) - lam * _normalised(a2[...], dv)).T
    ms = jnp.mean(o * o, axis=-1, keepdims=True)
    y = o * lax.rsqrt(ms + NORM_EPS) * gsub_ref[...]
    o_ref[...] = (y * (1.0 - lam_init)).astype(BF16)


def _diff_attention(proj, lq1, lk1, lq2, lk2, gsub, cast_weights, *, bsz, seq, diff_w, lam_init):
    t = min(DIFF_T, seq)
    dv = 2 * HEAD_DIM
    n_heads = diff_w // dv
    nq = seq // t
    kcol = diff_w // HEAD_DIM
    vcol = 2 * diff_w // dv
    vec = pl.BlockSpec((1, HEAD_DIM), lambda b, h, i: (0, 0))
    cast_in, cast_out, cast_shapes = _cast_rows_specs(cast_weights, lambda b, h, i: (b * n_heads + h) * nq + i)
    body = functools.partial(_diff_body, t=t, lam_init=lam_init, cast_rows=tuple(w.shape[0] for w in cast_weights))
    return pl.pallas_call(
        body,
        out_shape=(jax.ShapeDtypeStruct((bsz * seq, diff_w), BF16), *cast_shapes),
        grid=(bsz, n_heads, nq),
        in_specs=[
            vec, vec, vec, vec,
            pl.BlockSpec((1, dv), lambda b, h, i: (0, 0)),
            pl.BlockSpec((t, HEAD_DIM), lambda b, h, i: (b * nq + i, 2 * h)),
            pl.BlockSpec((t, HEAD_DIM), lambda b, h, i: (b * nq + i, 2 * h + 1)),
            pl.BlockSpec((seq, HEAD_DIM), lambda b, h, i: (b, kcol + 2 * h)),
            pl.BlockSpec((seq, HEAD_DIM), lambda b, h, i: (b, kcol + 2 * h + 1)),
            pl.BlockSpec((seq, dv), lambda b, h, i: (b, vcol + h)),
            *cast_in,
        ],
        out_specs=(pl.BlockSpec((t, dv), lambda b, h, i: (b * nq + i, h)), *cast_out),
        scratch_shapes=[pltpu.VMEM((dv + ONES_ROWS, seq), BF16),
                        pltpu.VMEM((1, t), F32), pltpu.VMEM((dv + ONES_ROWS, t), F32),
                        pltpu.VMEM((1, t), F32), pltpu.VMEM((dv + ONES_ROWS, t), F32)],
        compiler_params=_cparams(("arbitrary", "arbitrary", "arbitrary")),
        name="diffattn",
    )(lq1, lk1, lq2, lk2, gsub, proj, proj, proj, proj, proj, *cast_weights)


def _fox_body(g_ref, q_ref, k_ref, ck_ref, v_ref, *rest, t, cast_rows):
    n_cast = len(cast_rows)
    cast_in, (o_ref, *cast_out), (vt, ck_lanes, m, acc) = rest[:n_cast], rest[n_cast:2 * n_cast + 1], rest[2 * n_cast + 1:]
    _cast_rows(_linear_step(3), cast_in, cast_out, cast_rows)
    qi = pl.program_id(2)
    dv = v_ref.shape[1]

    @pl.when(qi == 0)
    def _():
        _build_vt(v_ref, vt)
        ck_lanes[...] = jnp.broadcast_to(ck_ref[...], ck_lanes.shape)

    m[...] = jnp.full(m.shape, NEG_BIG, F32)
    acc[...] = jnp.zeros(acc.shape, F32)
    q = q_ref[...]
    causal = lax.broadcasted_iota(jnp.int32, (t, t), 0) <= lax.broadcasted_iota(jnp.int32, (t, t), 1)

    def step(off, size, diagonal):
        ck = ck_lanes[pl.ds(off, size), :]
        s_t = _dot_nt(k_ref[pl.ds(off, size), :], q) - jnp.concatenate([ck] * (t // LANES), axis=1)
        if diagonal:
            s_t = jnp.where(causal, s_t, NEG_BIG)
        _softmax_step(s_t, vt[:, pl.ds(off, size)], m, acc)

    _sweep_keys(qi, t, step)

    o = _normalised(acc[...], dv).T
    ms = jnp.mean(o * o, axis=-1, keepdims=True)
    o_ref[...] = (o * lax.rsqrt(ms + NORM_EPS) * g_ref[...]).astype(BF16)


def _fox_attention(proj, c_cols, g_fox, cast_weights, *, bsz, seq, diff_w, fox_w):
    t = min(FOX_T, seq)
    n_heads = fox_w // HEAD_DIM
    nq = seq // t
    qcol = 3 * diff_w // HEAD_DIM
    kcol = qcol + n_heads
    vcol = kcol + n_heads
    cast_in, cast_out, cast_shapes = _cast_rows_specs(cast_weights, lambda b, h, i: (b * n_heads + h) * nq + i)
    body = functools.partial(_fox_body, t=t, cast_rows=tuple(w.shape[0] for w in cast_weights))
    return pl.pallas_call(
        body,
        out_shape=(jax.ShapeDtypeStruct((bsz * seq, fox_w), BF16), *cast_shapes),
        grid=(bsz, n_heads, nq),
        in_specs=[
            pl.BlockSpec((1, HEAD_DIM), lambda b, h, i: (0, 0)),
            pl.BlockSpec((t, HEAD_DIM), lambda b, h, i: (b * nq + i, qcol + h)),
            pl.BlockSpec((seq, HEAD_DIM), lambda b, h, i: (b, kcol + h)),
            pl.BlockSpec((None, seq, 1), lambda b, h, i: (b * n_heads + h, 0, 0)),
            pl.BlockSpec((seq, HEAD_DIM), lambda b, h, i: (b, vcol + h)),
            *cast_in,
        ],
        out_specs=(pl.BlockSpec((t, HEAD_DIM), lambda b, h, i: (b * nq + i, h)), *cast_out),
        scratch_shapes=[pltpu.VMEM((HEAD_DIM + ONES_ROWS, seq), BF16), pltpu.VMEM((seq, LANES), F32),
                        pltpu.VMEM((1, t), F32), pltpu.VMEM((HEAD_DIM + ONES_ROWS, t), F32)],
        compiler_params=_cparams(("arbitrary", "arbitrary", "arbitrary"), BIG_VMEM_LIMIT),
        name="foxattn",
    )(g_fox, proj, proj, c_cols, proj, *cast_weights)


def _outproj_body(od_ref, of_ref, wd_ref, wf_ref, x_ref, g_ref, wrh_ref, wrl_ref, br_ref, h_ref, hn_ref, lg_ref):
    h = (x_ref[...]
         + jnp.dot(od_ref[...], wd_ref[...], preferred_element_type=F32)
         + jnp.dot(of_ref[...], wf_ref[...], preferred_element_type=F32))
    h_ref[...] = h
    ms = jnp.mean(h * h, axis=-1, keepdims=True)
    hn = h * lax.rsqrt(ms + NORM_EPS) * g_ref[...]
    hn_ref[...] = hn
    hn_hi, hn_lo = _split_bf16(hn)
    lg_ref[...] = _dot3(hn_hi, hn_lo, wrh_ref[...], wrl_ref[...]) + br_ref[...]


def _outproj(od, of, w_od, w_of, x2, g_ffn, w_router_hi, w_router_lo, b_router_p):
    n, d = x2.shape
    tm = OUTPROJ_TM
    row = lambda i: (i, 0)
    const = lambda i: (0, 0)
    resident = functools.partial(pl.BlockSpec, index_map=const, pipeline_mode=pl.Buffered(1))
    return pl.pallas_call(
        _outproj_body,
        out_shape=(jax.ShapeDtypeStruct((n, d), F32), jax.ShapeDtypeStruct((n, d), F32),
                   jax.ShapeDtypeStruct((n, LANES), F32)),
        grid=(n // tm,),
        in_specs=[
            pl.BlockSpec((tm, od.shape[1]), row),
            pl.BlockSpec((tm, of.shape[1]), row),
            resident(w_od.shape),
            resident(w_of.shape),
            pl.BlockSpec((tm, d), row),
            pl.BlockSpec((1, d), const),
            resident((d, LANES)),
            resident((d, LANES)),
            pl.BlockSpec((1, LANES), const),
        ],
        out_specs=(pl.BlockSpec((tm, d), row), pl.BlockSpec((tm, d), row), pl.BlockSpec((tm, LANES), row)),
        compiler_params=_cparams(("parallel",)),
        name="outproj",
    )(od, of, w_od, w_of, x2, g_ffn, w_router_hi, w_router_lo, b_router_p)


def _route_body(lg_ref, e_ref, gate_ref, rank_ref, cnt_ref, carry_ref, *, n_experts):
    i = pl.program_id(0)
    tm = lg_ref.shape[0]

    @pl.when(i == 0)
    def _():
        carry_ref[...] = jnp.zeros(carry_ref.shape, F32)

    lane = lax.broadcasted_iota(jnp.int32, (tm, LANES), 1)
    logits = jnp.where(lane < n_experts, lg_ref[...], -jnp.inf)
    vals, idxs, hots = [], [], []
    for _ in range(TOP_K):
        mx = jnp.max(logits, axis=-1, keepdims=True)
        idx = jnp.min(jnp.where(logits == mx, lane, LANES), axis=-1, keepdims=True)
        hot = lane == idx
        logits = jnp.where(hot, -jnp.inf, logits)
        vals.append(mx)
        idxs.append(idx)
        hots.append(hot)

    exps = [jnp.exp(v - vals[0]) for v in vals]
    denom = exps[0]
    for ex in exps[1:]:
        denom = denom + ex
    inv = 1.0 / denom

    hot_sum = jnp.zeros((tm, LANES), F32)
    for hot in hots:
        hot_sum = hot_sum + jnp.where(hot, 1.0, 0.0)
    r = lax.broadcasted_iota(jnp.int32, (tm, tm), 0)
    c = lax.broadcasted_iota(jnp.int32, (tm, tm), 1)
    earlier = jnp.where(c < r, 1.0, 0.0).astype(BF16)
    base = carry_ref[...] + jnp.dot(earlier, hot_sum.astype(BF16), preferred_element_type=F32)

    e_out = jnp.zeros((tm, LANES), jnp.int32)
    g_out = jnp.zeros((tm, LANES), F32)
    r_out = jnp.zeros((tm, LANES), jnp.int32)
    for k in range(TOP_K):
        rank = jnp.sum(jnp.where(hots[k], base, 0.0), axis=-1, keepdims=True).astype(jnp.int32)
        e_out = jnp.where(lane == k, idxs[k], e_out)
        g_out = jnp.where(lane == k, exps[k] * inv, g_out)
        r_out = jnp.where(lane == k, rank, r_out)
    e_ref[...] = e_out
    gate_ref[...] = g_out
    rank_ref[...] = r_out
    carry_ref[...] = carry_ref[...] + jnp.sum(hot_sum, axis=0, keepdims=True)
    cnt_ref[...] = carry_ref[...]


def _route(logits, n_experts):
    n = logits.shape[0]
    tm = ROUTE_TM
    row = lambda i: (i, 0)
    body = functools.partial(_route_body, n_experts=n_experts)
    return pl.pallas_call(
        body,
        out_shape=(jax.ShapeDtypeStruct((n, LANES), jnp.int32), jax.ShapeDtypeStruct((n, LANES), F32),
                   jax.ShapeDtypeStruct((n, LANES), jnp.int32), jax.ShapeDtypeStruct((1, LANES), F32)),
        grid=(n // tm,),
        in_specs=[pl.BlockSpec((tm, LANES), row)],
        out_specs=(pl.BlockSpec((tm, LANES), row), pl.BlockSpec((tm, LANES), row),
                   pl.BlockSpec((tm, LANES), row), pl.BlockSpec((1, LANES), lambda i: (0, 0))),
        scratch_shapes=[pltpu.VMEM((1, LANES), F32)],
        compiler_params=_cparams(("arbitrary",)),
        name="route",
    )(logits)


def _dispatch_body(ztile_ref, dest_ref, hn_ref, xs_ref, zeros, sem, zsem):
    tm = hn_ref.shape[0]

    @pl.when(pl.program_id(0) == 0)
    def _():
        zeros[...] = jnp.zeros(zeros.shape, zeros.dtype)

        def fill(z):
            row0 = pl.multiple_of(ztile_ref[z] * EXPERT_TM, EXPERT_TM)
            return pltpu.make_async_copy(zeros, xs_ref.at[pl.ds(row0, EXPERT_TM), :], zsem)

        def fill_start(z, carry):
            @pl.when(ztile_ref[z] >= 0)
            def _():
                fill(z).start()
            return carry

        def fill_wait(z, carry):
            @pl.when(ztile_ref[z] >= 0)
            def _():
                fill(z).wait()
            return carry

        lax.fori_loop(0, ztile_ref.shape[0], fill_start, 0)
        lax.fori_loop(0, ztile_ref.shape[0], fill_wait, 0)

    def start(t, carry):
        src = hn_ref.at[pl.ds(t, 1), :]
        for k in range(TOP_K):
            pltpu.make_async_copy(src, xs_ref.at[pl.ds(dest_ref[0, t * TOP_K + k], 1), :], sem).start()
        return carry

    lax.fori_loop(0, tm, start, 0)
    for _ in range(TOP_K):
        pltpu.make_async_copy(hn_ref, xs_ref.at[pl.ds(0, tm), :], sem).wait()


def _dispatch(zero_tiles, dest, hn, n_rows):
    n, d = hn.shape
    tm = DISPATCH_TM
    dest3 = dest.reshape(n // tm, 1, tm * TOP_K)
    return pl.pallas_call(
        _dispatch_body,
        out_shape=jax.ShapeDtypeStruct((n_rows, d), hn.dtype),
        grid_spec=pltpu.PrefetchScalarGridSpec(
            num_scalar_prefetch=1,
            grid=(n // tm,),
            in_specs=[
                pl.BlockSpec((None, 1, tm * TOP_K), lambda i, zt: (i, 0, 0), memory_space=pltpu.SMEM),
                pl.BlockSpec((tm, d), lambda i, zt: (i, 0)),
            ],
            out_specs=pl.BlockSpec(memory_space=pl.ANY),
            scratch_shapes=[pltpu.VMEM((EXPERT_TM, d), hn.dtype), pltpu.SemaphoreType.DMA,
                            pltpu.SemaphoreType.DMA],
        ),
        compiler_params=_cparams(("arbitrary",)),
        name="dispatch",
    )(zero_tiles, dest3, hn)


def _experts_body(te_ref, nu_ref, live_ref, x_ref, wg_ref, bg_ref, wu_ref, bu_ref, wd_ref, bd_ref, o_ref, xb_ref):
    i = pl.program_id(0)
    j = pl.program_id(1)
    tm = x_ref.shape[0]

    def compute(r):
        @pl.when(j == 0)
        def _():
            xb_ref[0:r, :] = x_ref[0:r, :].astype(BF16)
            o_ref[0:r, :] = jnp.broadcast_to(bd_ref[...], (r, o_ref.shape[1]))
            if r < tm:
                o_ref[r:tm, :] = jnp.zeros((tm - r, o_ref.shape[1]), F32)

        xb = xb_ref[0:r, :]
        g = jnp.dot(xb, wg_ref[...], preferred_element_type=F32) + bg_ref[...]
        u = jnp.dot(xb, wu_ref[...], preferred_element_type=F32) + bu_ref[...]
        g = jnp.minimum(g, SWIGLU_LIMIT)
        u = jnp.clip(u, -SWIGLU_LIMIT, SWIGLU_LIMIT)
        hdn = g * (1.0 / (1.0 + jnp.exp(-SWIGLU_ALPHA * g))) * (u + 1.0)
        o_ref[0:r, :] += jnp.dot(hdn.astype(BF16), wd_ref[...], preferred_element_type=F32)

    for r in range(EXPERT_RT, tm + 1, EXPERT_RT):
        pl.when(live_ref[i] == r)(functools.partial(compute, r))

    @pl.when((live_ref[i] == 0) & (j == 0))
    def _():
        o_ref[...] = jnp.zeros(o_ref.shape, F32)


def _experts(tile_expert, n_used, tile_live, xs, wg, bg, wu, bu, wd, bd):
    rows, d = xs.shape
    d_exp = wg.shape[2]
    tm, tc = EXPERT_TM, min(EXPERT_TC, d_exp)
    n_tiles = rows // tm
    nj = d_exp // tc

    def tile(i, nu):
        return jnp.minimum(i, nu[0] - 1)

    def chunk(i, j, nu):
        return jnp.where(i < nu[0], j, nj - 1)

    return pl.pallas_call(
        _experts_body,
        out_shape=jax.ShapeDtypeStruct((rows, d), F32),
        grid_spec=pltpu.PrefetchScalarGridSpec(
            num_scalar_prefetch=3,
            grid=(n_tiles, nj),
            in_specs=[
                pl.BlockSpec((tm, d), lambda i, j, te, nu, live: (tile(i, nu), 0)),
                pl.BlockSpec((None, d, tc), lambda i, j, te, nu, live: (te[i], 0, chunk(i, j, nu))),
                pl.BlockSpec((None, 1, tc), lambda i, j, te, nu, live: (te[i], 0, chunk(i, j, nu))),
                pl.BlockSpec((None, d, tc), lambda i, j, te, nu, live: (te[i], 0, chunk(i, j, nu))),
                pl.BlockSpec((None, 1, tc), lambda i, j, te, nu, live: (te[i], 0, chunk(i, j, nu))),
                pl.BlockSpec((None, tc, d), lambda i, j, te, nu, live: (te[i], chunk(i, j, nu), 0)),
                pl.BlockSpec((None, 1, d), lambda i, j, te, nu, live: (te[i], 0, 0)),
            ],
            out_specs=pl.BlockSpec((tm, d), lambda i, j, te, nu, live: (i, 0)),
            scratch_shapes=[pltpu.VMEM((tm, d), BF16)],
        ),
        compiler_params=_cparams(("arbitrary", "arbitrary"), BIG_VMEM_LIMIT),
        name="experts",
    )(tile_expert, n_used, tile_live, xs, wg, bg, wu, bu, wd, bd)


def _combine_body(dcur_ref, dnext_ref, gate_ref, h_ref, gfin_ref, ys_ref, o_ref, buf, sems, *, final_norm):
    i = pl.program_id(0)
    nb = pl.num_programs(0)
    tm = h_ref.shape[0]
    slot = i % 2

    def start_all(dref, s):
        def go(t, carry):
            for k in range(TOP_K):
                pltpu.make_async_copy(ys_ref.at[pl.ds(dref[0, t * TOP_K + k], 1), :],
                                      buf.at[s, k, pl.ds(t, 1), :], sems.at[s]).start()
            return carry
        lax.fori_loop(0, tm, go, 0)

    @pl.when(i == 0)
    def _():
        start_all(dcur_ref, 0)

    @pl.when(i + 1 < nb)
    def _():
        start_all(dnext_ref, 1 - slot)

    for k in range(TOP_K):
        pltpu.make_async_copy(ys_ref.at[pl.ds(0, tm), :], buf.at[slot, k], sems.at[slot]).wait()

    y = h_ref[...]
    gates = gate_ref[...]
    for k in range(TOP_K):
        y = y + gates[:, k:k + 1] * buf[slot, k]
    if final_norm:
        ms = jnp.mean(y * y, axis=-1, keepdims=True)
        y = y * lax.rsqrt(ms + NORM_EPS) * gfin_ref[...]
    o_ref[...] = y


def _combine(dest, gates, h, g_final, ys, *, final_norm):
    n, d = h.shape
    tm = COMBINE_TM
    nb = n // tm
    dest3 = dest.reshape(nb, 1, tm * TOP_K)
    smem_blk = (None, 1, tm * TOP_K)
    return pl.pallas_call(
        functools.partial(_combine_body, final_norm=final_norm),
        out_shape=jax.ShapeDtypeStruct((n, d), F32),
        grid=(nb,),
        in_specs=[
            pl.BlockSpec(smem_blk, lambda i: (i, 0, 0), memory_space=pltpu.SMEM),
            pl.BlockSpec(smem_blk, lambda i: (jnp.minimum(i + 1, nb - 1), 0, 0), memory_space=pltpu.SMEM),
            pl.BlockSpec((tm, LANES), lambda i: (i, 0)),
            pl.BlockSpec((tm, d), lambda i: (i, 0)),
            pl.BlockSpec((1, d), lambda i: (0, 0)),
            pl.BlockSpec(memory_space=pl.ANY),
        ],
        out_specs=pl.BlockSpec((tm, d), lambda i: (i, 0)),
        scratch_shapes=[pltpu.VMEM((2, TOP_K, tm, d), F32), pltpu.SemaphoreType.DMA((2,))],
        compiler_params=_cparams(("arbitrary",)),
        name="combine",
    )(dest3, dest3, gates, h, g_final, ys)


def _rope_tables(positions):
    inv_freq = ROPE_THETA ** (-jnp.arange(0, HEAD_DIM, 2, dtype=F32) / HEAD_DIM)
    ang = positions.astype(F32)[..., None] * inv_freq
    ang = jnp.concatenate([ang, ang], axis=-1)
    sign = jnp.concatenate([-jnp.ones((HEAD_DIM // 2,), F32), jnp.ones((HEAD_DIM // 2,), F32)])
    return jnp.cos(ang), jnp.sin(ang) * sign


def _layer(x2, cos, sin_signed, layer_idx, bsz, seq, g_attn_norm, w_in, b_forget, lambda_q1, lambda_k1,
           lambda_q2, lambda_k2, g_diff_sub, g_fox_out, w_out, g_ffn_norm, w_router, b_router,
           w_gate, b_gate, w_up, b_up, w_down, b_down):
    n, d = x2.shape
    diff_w = d // 2
    fox_w = d - diff_w
    fox_heads = fox_w // HEAD_DIM
    main_cols = 3 * diff_w + 3 * fox_w
    n_experts = w_router.shape[1]
    lam_init = _lambda_init(layer_idx)

    d_exp = w_gate.shape[2]
    hosted = {"inproj": w_up.reshape(n_experts * d, d_exp),
              "diff": w_down.reshape(n_experts * d_exp, d),
              "fox": w_gate.reshape(n_experts * d, d_exp)}
    host_steps = {"inproj": (n // min(INPROJ_TM, n)) * (main_cols // INPROJ_TN),
                  "diff": bsz * (diff_w // (2 * HEAD_DIM)) * (seq // min(DIFF_T, seq)),
                  "fox": bsz * fox_heads * (seq // min(FOX_T, seq))}
    fused = {k: _can_cast_rows((w,), host_steps[k]) for k, w in hosted.items()}
    cast_arg = {k: (w,) if fused[k] else () for k, w in hosted.items()}

    w_main = w_in[:, :main_cols].astype(BF16)
    w_f_hi, w_f_lo = _split_bf16(jnp.pad(w_in[:, main_cols:], ((0, 0), (0, LANES - fox_heads))))
    proj, f_logit, *up_b = _inproj(x2, g_attn_norm[None, :], w_main, w_f_hi, w_f_lo, cos, sin_signed,
                                   cast_arg["inproj"], diff_w=diff_w, fox_w=fox_w)

    z_rows = f_logit[:, :fox_heads].reshape(bsz, seq, fox_heads).transpose(0, 2, 1).reshape(bsz * fox_heads, seq)
    b_rows = jnp.tile(b_forget.astype(F32), bsz)[:, None]
    c_rows = _cumgate(z_rows, b_rows)

    od, *down_b = _diff_attention(proj, lambda_q1[None, :], lambda_k1[None, :], lambda_q2[None, :],
                                  lambda_k2[None, :], g_diff_sub[None, :], cast_arg["diff"],
                                  bsz=bsz, seq=seq, diff_w=diff_w, lam_init=lam_init)
    of, *gate_b = _fox_attention(proj, (c_rows * LOG2E)[:, :, None], g_fox_out[None, :], cast_arg["fox"],
                                 bsz=bsz, seq=seq, diff_w=diff_w, fox_w=fox_w)
    w_up_b = (up_b[0] if fused["inproj"] else hosted["inproj"].astype(BF16)).reshape(w_up.shape)
    w_down_b = (down_b[0] if fused["diff"] else hosted["diff"].astype(BF16)).reshape(w_down.shape)
    w_gate_b = (gate_b[0] if fused["fox"] else hosted["fox"].astype(BF16)).reshape(w_gate.shape)

    w_out_b = w_out.astype(BF16)
    w_router_hi, w_router_lo = _split_bf16(jnp.pad(w_router, ((0, 0), (0, LANES - n_experts))))
    b_router_p = jnp.pad(b_router, (0, LANES - n_experts))[None, :]
    h, hn, logits = _outproj(od, of, w_out_b[:diff_w], w_out_b[diff_w:], x2, g_ffn_norm[None, :],
                             w_router_hi, w_router_lo, b_router_p)

    e_pad, gate_pad, rank_pad, cnt = _route(logits, n_experts)
    e_idx = e_pad[:, :TOP_K]
    counts = cnt[0, :n_experts].astype(jnp.int32)
    padded = (counts + EXPERT_TM - 1) // EXPERT_TM * EXPERT_TM
    pad_ends = jnp.cumsum(padded)
    pad_starts = pad_ends - padded
    start_of = jnp.sum(jnp.where(e_idx[..., None] == jnp.arange(n_experts), pad_starts, 0), axis=-1)
    dest = (start_of + rank_pad[:, :TOP_K]).astype(jnp.int32)
    n_tiles = -(-n * TOP_K // EXPERT_TM) + n_experts
    tile_start = jnp.arange(n_tiles, dtype=jnp.int32) * EXPERT_TM
    tile_expert = jnp.sum((pad_ends[None, :] <= tile_start[:, None]).astype(jnp.int32), axis=1)
    tile_expert = jnp.minimum(tile_expert, n_experts - 1)
    n_used = (pad_ends[-1:] // EXPERT_TM).astype(jnp.int32)
    tile_expert = jnp.where(jnp.arange(n_tiles) < n_used[0], tile_expert, tile_expert[n_used[0] - 1])
    last_tile = jnp.where(padded > 0, pad_ends // EXPERT_TM - 1, -1)
    tail_tile = n_used[0] + jnp.arange(n_tiles - n * TOP_K // EXPERT_TM, dtype=jnp.int32)
    tail_tile = jnp.where(tail_tile < n_tiles, tail_tile, -1)
    zero_tiles = jnp.concatenate([last_tile, tail_tile]).astype(jnp.int32)

    routed_end = (pad_starts + counts)[tile_expert]
    tile_live = jnp.clip(routed_end - tile_start, 0, EXPERT_TM)
    tile_live = jnp.where(jnp.arange(n_tiles) < n_used[0], tile_live, 0)
    tile_live = ((tile_live + EXPERT_RT - 1) // EXPERT_RT * EXPERT_RT).astype(jnp.int32)

    xs = _dispatch(zero_tiles, dest, hn, n_tiles * EXPERT_TM)
    ys = _experts(tile_expert, n_used, tile_live, xs, w_gate_b, b_gate[:, None, :], w_up_b, b_up[:, None, :],
                  w_down_b, b_down[:, None, :])
    return dest, gate_pad, h, ys


def kernel(x, positions, g_attn_norm, w_in, b_forget, lambda_q1, lambda_k1, lambda_q2, lambda_k2, g_diff_sub,
           g_fox_out, w_out, g_ffn_norm, w_router, b_router, w_gate, b_gate, w_up, b_up, w_down, b_down,
           g_final):
    bsz, seq, d = x.shape
    depth = w_in.shape[0]
    cos, sin_signed = _rope_tables(positions)
    cos = cos.reshape(bsz * seq, HEAD_DIM)
    sin_signed = sin_signed.reshape(bsz * seq, HEAD_DIM)
    x2 = x.reshape(bsz * seq, d)
    for l in range(depth):
        dest, gates, h, ys = _layer(
            x2, cos, sin_signed, l, bsz, seq, g_attn_norm[l], w_in[l], b_forget[l], lambda_q1[l], lambda_k1[l],
            lambda_q2[l], lambda_k2[l], g_diff_sub[l], g_fox_out[l], w_out[l], g_ffn_norm[l], w_router[l],
            b_router[l], w_gate[l], b_gate[l], w_up[l], b_up[l], w_down[l], b_down[l])
        x2 = _combine(dest, gates, h, g_final[None, :], ys, final_norm=(l == depth - 1))
    return x2.reshape(bsz, seq, d)
```

```python
import functools
import math

import jax
import jax.numpy as jnp
from jax import lax
from jax.experimental import pallas as pl
from jax.experimental.pallas import tpu as pltpu

F32 = jnp.float32
BF16 = jnp.bfloat16

HEAD_DIM = 128
CHUNK = 64
ROPE_THETA = 10000.0
NORM_EPS = 1e-5
NEG_BIG = -1e30
TOP_K = 4
SWIGLU_LIMIT = 7.0
SWIGLU_ALPHA = 1.702
LANES = 128
SUBLANES = 8
QK_SCALE = HEAD_DIM ** -0.5
LOG2E = math.log2(math.e)
ONES_ROWS = 16

INPROJ_TM = 1024
INPROJ_TN = 512
DIFF_T = 512
FOX_T = 1024
CAST_ROWS = 512
VT_CHUNK = 512
OUTPROJ_TM = 512
ROUTE_TM = 512
DISPATCH_TM = 256
EXPERT_TM = 512
EXPERT_RT = 128
EXPERT_TC = 1024
COMBINE_TM = 128
VMEM_LIMIT = 48 * 1024 * 1024
BIG_VMEM_LIMIT = 58 * 1024 * 1024


def _lambda_init(layer_idx):
    return 0.8 - 0.6 * math.exp(-0.3 * layer_idx)


def _cparams(sem, vmem_limit=VMEM_LIMIT):
    return pltpu.CompilerParams(dimension_semantics=sem, vmem_limit_bytes=vmem_limit)


def _split_bf16(x):
    hi = x.astype(BF16)
    return hi, (x - hi.astype(F32)).astype(BF16)


def _dot_nt(a, b):
    return lax.dot_general(a, b, (((1,), (1,)), ((), ())), preferred_element_type=F32)


def _dot3(x_hi, x_lo, w_hi, w_lo):
    dot = functools.partial(jnp.dot, preferred_element_type=F32)
    return dot(x_hi, w_hi) + (dot(x_hi, w_lo) + dot(x_lo, w_hi))


def _cast_rows_specs(weights, step_of):
    in_specs, out_specs, out_shapes = [], [], []
    for w in weights:
        rows, cols = w.shape
        last = rows // CAST_ROWS - 1
        spec = pl.BlockSpec((CAST_ROWS, cols), lambda *idx, last=last: (jnp.minimum(step_of(*idx), last), 0))
        in_specs.append(spec)
        out_specs.append(spec)
        out_shapes.append(jax.ShapeDtypeStruct((rows, cols), BF16))
    return in_specs, out_specs, out_shapes


def _can_cast_rows(weights, steps):
    return all(w.shape[0] % CAST_ROWS == 0 and w.shape[0] // CAST_ROWS <= steps for w in weights)


def _linear_step(n_axes):
    step = pl.program_id(0)
    for axis in range(1, n_axes):
        step = step * pl.num_programs(axis) + pl.program_id(axis)
    return step


def _cast_rows(step, in_refs, out_refs, total_rows):
    for src, dst, rows in zip(in_refs, out_refs, total_rows):
        @pl.when(step < rows // CAST_ROWS)
        def _():
            dst[...] = src[...].astype(BF16)


def _inproj_body(x_ref, g_ref, w_ref, wfh_ref, wfl_ref, cos_ref, sin_ref, *rest, n_rope, tn, q_ranges, cast_rows):
    n_cast = len(cast_rows)
    cast_in, (o_ref, fl_ref, *cast_out), (xn_ref,) = rest[:n_cast], rest[n_cast:2 * n_cast + 2], rest[2 * n_cast + 2:]
    _cast_rows(_linear_step(2), cast_in, cast_out, cast_rows)
    j = pl.program_id(1)

    @pl.when(j == 0)
    def _():
        x = x_ref[...]
        ms = jnp.mean(x * x, axis=-1, keepdims=True)
        xn = x * lax.rsqrt(ms + NORM_EPS) * g_ref[...]
        xn_hi, xn_lo = _split_bf16(xn)
        xn_ref[...] = xn_hi
        fl_ref[...] = _dot3(xn_hi, xn_lo, wfh_ref[...], wfl_ref[...])

    acc = _dot_nt(xn_ref[...], w_ref[...])
    col0 = j * tn
    is_q = (col0 >= q_ranges[0][0]) & (col0 < q_ranges[0][1])
    is_q = is_q | ((col0 >= q_ranges[1][0]) & (col0 < q_ranges[1][1]))
    scale = jnp.where(is_q, QK_SCALE * LOG2E, 1.0).astype(F32)

    @pl.when(j < n_rope)
    def _():
        cos = cos_ref[...] * scale
        sin = sin_ref[...] * scale
        for gi in range(tn // LANES):
            xg = acc[:, gi * LANES:(gi + 1) * LANES]
            rot = pltpu.roll(xg, HEAD_DIM // 2, 1)
            o_ref[:, gi * LANES:(gi + 1) * LANES] = (xg * cos + rot * sin).astype(BF16)

    @pl.when(j >= n_rope)
    def _():
        o_ref[...] = (acc * scale).astype(BF16)


def _inproj(x2, g, w_main_t, w_f_hi, w_f_lo, cos, sin_signed, cast_weights, *, diff_w, fox_w):
    n, d = x2.shape
    cols = w_main_t.shape[0]
    tm, tn = min(INPROJ_TM, n), INPROJ_TN
    n_rope = 2 * diff_w // tn
    q_ranges = ((0, diff_w), (3 * diff_w, 3 * diff_w + fox_w))
    cast_in, cast_out, cast_shapes = _cast_rows_specs(cast_weights, lambda i, j: i * (cols // tn) + j)
    body = functools.partial(_inproj_body, n_rope=n_rope, tn=tn, q_ranges=q_ranges,
                             cast_rows=tuple(w.shape[0] for w in cast_weights))
    return pl.pallas_call(
        body,
        out_shape=(jax.ShapeDtypeStruct((n, cols), BF16), jax.ShapeDtypeStruct((n, LANES), F32), *cast_shapes),
        grid=(n // tm, cols // tn),
        in_specs=[
            pl.BlockSpec((tm, d), lambda i, j: (i, 0)),
            pl.BlockSpec((1, d), lambda i, j: (0, 0)),
            pl.BlockSpec((tn, d), lambda i, j: (j, 0)),
            pl.BlockSpec((d, LANES), lambda i, j: (0, 0)),
            pl.BlockSpec((d, LANES), lambda i, j: (0, 0)),
            pl.BlockSpec((tm, LANES), lambda i, j: (i, 0)),
            pl.BlockSpec((tm, LANES), lambda i, j: (i, 0)),
            *cast_in,
        ],
        out_specs=(pl.BlockSpec((tm, tn), lambda i, j: (i, j)),
                   pl.BlockSpec((tm, LANES), lambda i, j: (i, 0)), *cast_out),
        scratch_shapes=[pltpu.VMEM((tm, d), BF16)],
        compiler_params=_cparams(("arbitrary", "arbitrary"), BIG_VMEM_LIMIT),
        name="inproj",
    )(x2, g, w_main_t, w_f_hi, w_f_lo, cos, sin_signed, *cast_weights)


def _cumgate_body(z_ref, b_ref, c_ref):
    z = z_ref[...] + b_ref[...]
    v = jnp.minimum(z, 0.0) - jnp.log(1.0 + jnp.exp(-jnp.abs(z)))
    seq = v.shape[1]
    lane = lax.broadcasted_iota(jnp.int32, v.shape, 1)
    shift = 1
    while shift < seq:
        v = v + jnp.where(lane >= shift, pltpu.roll(v, shift, 1), 0.0)
        shift *= 2
    c_ref[...] = v


def _cumgate(z_rows, b_rows):
    return pl.pallas_call(
        _cumgate_body,
        out_shape=jax.ShapeDtypeStruct(z_rows.shape, F32),
        name="cumgate",
    )(z_rows, b_rows)


def _build_vt(v_ref, vt_ref):
    seq, dv = v_ref.shape
    for c in range(seq // VT_CHUNK):
        blk = v_ref[c * VT_CHUNK:(c + 1) * VT_CHUNK, :].astype(F32)
        vt_ref[0:dv, c * VT_CHUNK:(c + 1) * VT_CHUNK] = blk.T.astype(BF16)
    vt_ref[dv:dv + ONES_ROWS, :] = jnp.ones((ONES_ROWS, seq), BF16)


def _softmax_step(s_t, vt_blk, m_ref, acc_ref):
    m_prev = m_ref[...]
    m_new = jnp.maximum(m_prev, jnp.max(s_t, axis=0, keepdims=True))
    alpha = jnp.exp2(m_prev - m_new)
    p_t = jnp.exp2(s_t - m_new).astype(BF16)
    acc_ref[...] = alpha * acc_ref[...] + jnp.dot(vt_blk, p_t, preferred_element_type=F32)
    m_ref[...] = m_new


def _sweep_keys(qi, t, step):
    def pair(i, carry):
        step(pl.multiple_of(i * (2 * t), 2 * t), 2 * t, False)
        return carry

    lax.fori_loop(0, qi // 2, pair, 0)

    @pl.when(qi % 2 == 1)
    def _():
        step(pl.multiple_of((qi - 1) * t, t), t, False)

    step(pl.multiple_of(qi * t, t), t, True)


def _normalised(acc, dv):
    return acc[0:dv] * (1.0 / acc[dv:dv + 1])


def _diff_body(lq1_ref, lk1_ref, lq2_ref, lk2_ref, gsub_ref, q1_ref, q2_ref, k1_ref, k2_ref, v_ref,
               *rest, t, lam_init, cast_rows):
    n_cast = len(cast_rows)
    cast_in, (o_ref, *cast_out), (vt, m1, a1, m2, a2) = rest[:n_cast], rest[n_cast:2 * n_cast + 1], rest[2 * n_cast + 1:]
    _cast_rows(_linear_step(3), cast_in, cast_out, cast_rows)
    qi = pl.program_id(2)
    dv = v_ref.shape[1]

    @pl.when(qi == 0)
    def _():
        _build_vt(v_ref, vt)

    for m_ref, a_ref in ((m1, a1), (m2, a2)):
        m_ref[...] = jnp.full(m_ref.shape, NEG_BIG, F32)
        a_ref[...] = jnp.zeros(a_ref.shape, F32)
    q1 = q1_ref[...]
    q2 = q2_ref[...]
    key_chunk = lax.broadcasted_iota(jnp.int32, (t, t), 0) // CHUNK
    qry_chunk = lax.broadcasted_iota(jnp.int32, (t, t), 1) // CHUNK
    visible = key_chunk <= qry_chunk

    def step(off, size, diagonal):
        vt_blk = vt[:, pl.ds(off, size)]
        for q, k_ref, m_ref, a_ref in ((q1, k1_ref, m1, a1), (q2, k2_ref, m2, a2)):
            s_t = _dot_nt(k_ref[pl.ds(off, size), :], q)
            if diagonal:
                s_t = jnp.where(visible, s_t, NEG_BIG)
            _softmax_step(s_t, vt_blk, m_ref, a_ref)

    _sweep_keys(qi, t, step)

    lam = (jnp.exp(jnp.sum(lq1_ref[...] * lk1_ref[...], axis=-1, keepdims=True))
           - jnp.exp(jnp.sum(lq2_ref[...] * lk2_ref[...], axis=-1, keepdims=True)) + lam_init)
    o = (_normalised(a1[...], dv) - lam * _normalised(a2[...], dv)).T
    ms = jnp.mean(o * o, axis=-1, keepdims=True)
    y = o * lax.rsqrt(ms + NORM_EPS) * gsub_ref[...]
    o_ref[...] = (y * (1.0 - lam_init)).astype(BF16)


def _diff_attention(proj, lq1, lk1, lq2, lk2, gsub, cast_weights, *, bsz, seq, diff_w, lam_init):
    t = min(DIFF_T, seq)
    dv = 2 * HEAD_DIM
    n_heads = diff_w // dv
    nq = seq // t
    kcol = diff_w // HEAD_DIM
    vcol = 2 * diff_w // dv
    vec = pl.BlockSpec((1, HEAD_DIM), lambda b, h, i: (0, 0))
    cast_in, cast_out, cast_shapes = _cast_rows_specs(cast_weights, lambda b, h, i: (b * n_heads + h) * nq + i)
    body = functools.partial(_diff_body, t=t, lam_init=lam_init, cast_rows=tuple(w.shape[0] for w in cast_weights))
    return pl.pallas_call(
        body,
        out_shape=(jax.ShapeDtypeStruct((bsz * seq, diff_w), BF16), *cast_shapes),
        grid=(bsz, n_heads, nq),
        in_specs=[
            vec, vec, vec, vec,
            pl.BlockSpec((1, dv), lambda b, h, i: (0, 0)),
            pl.BlockSpec((t, HEAD_DIM), lambda b, h, i: (b * nq + i, 2 * h)),
            pl.BlockSpec((t, HEAD_DIM), lambda b, h, i: (b * nq + i, 2 * h + 1)),
            pl.BlockSpec((seq, HEAD_DIM), lambda b, h, i: (b, kcol + 2 * h)),
            pl.BlockSpec((seq, HEAD_DIM), lambda b, h, i: (b, kcol + 2 * h + 1)),
            pl.BlockSpec((seq, dv), lambda b, h, i: (b, vcol + h)),
            *cast_in,
        ],
        out_specs=(pl.BlockSpec((t, dv), lambda b, h, i: (b * nq + i, h)), *cast_out),
        scratch_shapes=[pltpu.VMEM((dv + ONES_ROWS, seq), BF16),
                        pltpu.VMEM((1, t), F32), pltpu.VMEM((dv + ONES_ROWS, t), F32),
                        pltpu.VMEM((1, t), F32), pltpu.VMEM((dv + ONES_ROWS, t), F32)],
        compiler_params=_cparams(("arbitrary", "arbitrary", "arbitrary"), BIG_VMEM_LIMIT),
        name="diffattn",
    )(lq1, lk1, lq2, lk2, gsub, proj, proj, proj, proj, proj, *cast_weights)


def _fox_body(g_ref, q_ref, k_ref, ck_ref, v_ref, *rest, t, cast_rows):
    n_cast = len(cast_rows)
    cast_in, (o_ref, *cast_out), (vt, ck_lanes, m, acc) = rest[:n_cast], rest[n_cast:2 * n_cast + 1], rest[2 * n_cast + 1:]
    _cast_rows(_linear_step(3), cast_in, cast_out, cast_rows)
    qi = pl.program_id(2)
    dv = v_ref.shape[1]

    @pl.when(qi == 0)
    def _():
        _build_vt(v_ref, vt)
        for j in range(ck_ref.shape[0]):
            ck_lanes[j * LANES:(j + 1) * LANES, :] = jnp.broadcast_to(ck_ref[j:j + 1, :], (LANES, LANES)).T

    m[...] = jnp.full(m.shape, NEG_BIG, F32)
    acc[...] = jnp.zeros(acc.shape, F32)
    q = q_ref[...]
    causal = lax.broadcasted_iota(jnp.int32, (t, t), 0) <= lax.broadcasted_iota(jnp.int32, (t, t), 1)

    def step(off, size, diagonal):
        ck = ck_lanes[pl.ds(off, size), :]
        s_t = _dot_nt(k_ref[pl.ds(off, size), :], q) - jnp.concatenate([ck] * (t // LANES), axis=1)
        if diagonal:
            s_t = jnp.where(causal, s_t, NEG_BIG)
        _softmax_step(s_t, vt[:, pl.ds(off, size)], m, acc)

    _sweep_keys(qi, t, step)

    o = _normalised(acc[...], dv).T
    ms = jnp.mean(o * o, axis=-1, keepdims=True)
    o_ref[...] = (o * lax.rsqrt(ms + NORM_EPS) * g_ref[...]).astype(BF16)


def _fox_attention(proj, c_keys, g_fox, cast_weights, *, bsz, seq, diff_w, fox_w):
    t = min(FOX_T, seq)
    n_heads = fox_w // HEAD_DIM
    nq = seq // t
    qcol = 3 * diff_w // HEAD_DIM
    kcol = qcol + n_heads
    vcol = kcol + n_heads
    cast_in, cast_out, cast_shapes = _cast_rows_specs(cast_weights, lambda b, h, i: (b * n_heads + h) * nq + i)
    body = functools.partial(_fox_body, t=t, cast_rows=tuple(w.shape[0] for w in cast_weights))
    return pl.pallas_call(
        body,
        out_shape=(jax.ShapeDtypeStruct((bsz * seq, fox_w), BF16), *cast_shapes),
        grid=(bsz, n_heads, nq),
        in_specs=[
            pl.BlockSpec((1, HEAD_DIM), lambda b, h, i: (0, 0)),
            pl.BlockSpec((t, HEAD_DIM), lambda b, h, i: (b * nq + i, qcol + h)),
            pl.BlockSpec((seq, HEAD_DIM), lambda b, h, i: (b, kcol + h)),
            pl.BlockSpec((None, seq // LANES, LANES), lambda b, h, i: (b * n_heads + h, 0, 0)),
            pl.BlockSpec((seq, HEAD_DIM), lambda b, h, i: (b, vcol + h)),
            *cast_in,
        ],
        out_specs=(pl.BlockSpec((t, HEAD_DIM), lambda b, h, i: (b * nq + i, h)), *cast_out),
        scratch_shapes=[pltpu.VMEM((HEAD_DIM + ONES_ROWS, seq), BF16), pltpu.VMEM((seq, LANES), F32),
                        pltpu.VMEM((1, t), F32), pltpu.VMEM((HEAD_DIM + ONES_ROWS, t), F32)],
        compiler_params=_cparams(("arbitrary", "arbitrary", "arbitrary"), BIG_VMEM_LIMIT),
        name="foxattn",
    )(g_fox, proj, proj, c_keys, proj, *cast_weights)


def _outproj_body(od_ref, of_ref, wd_ref, wf_ref, x_ref, g_ref, wrh_ref, wrl_ref, br_ref, h_ref, hn_ref, lg_ref):
    h = (x_ref[...]
         + jnp.dot(od_ref[...], wd_ref[...], preferred_element_type=F32)
         + jnp.dot(of_ref[...], wf_ref[...], preferred_element_type=F32))
    h_ref[...] = h
    ms = jnp.mean(h * h, axis=-1, keepdims=True)
    hn = h * lax.rsqrt(ms + NORM_EPS) * g_ref[...]
    hn_ref[...] = hn
    hn_hi, hn_lo = _split_bf16(hn)
    lg_ref[...] = _dot3(hn_hi, hn_lo, wrh_ref[...], wrl_ref[...]) + br_ref[...]


def _outproj(od, of, w_od, w_of, x2, g_ffn, w_router_hi, w_router_lo, b_router_p):
    n, d = x2.shape
    tm = OUTPROJ_TM
    row = lambda i: (i, 0)
    const = lambda i: (0, 0)
    resident = functools.partial(pl.BlockSpec, index_map=const, pipeline_mode=pl.Buffered(1))
    return pl.pallas_call(
        _outproj_body,
        out_shape=(jax.ShapeDtypeStruct((n, d), F32), jax.ShapeDtypeStruct((n, d), F32),
                   jax.ShapeDtypeStruct((n, LANES), F32)),
        grid=(n // tm,),
        in_specs=[
            pl.BlockSpec((tm, od.shape[1]), row),
            pl.BlockSpec((tm, of.shape[1]), row),
            resident(w_od.shape),
            resident(w_of.shape),
            pl.BlockSpec((tm, d), row),
            pl.BlockSpec((1, d), const),
            resident((d, LANES)),
            resident((d, LANES)),
            pl.BlockSpec((1, LANES), const),
        ],
        out_specs=(pl.BlockSpec((tm, d), row), pl.BlockSpec((tm, d), row), pl.BlockSpec((tm, LANES), row)),
        compiler_params=_cparams(("parallel",)),
        name="outproj",
    )(od, of, w_od, w_of, x2, g_ffn, w_router_hi, w_router_lo, b_router_p)


def _route_body(lg_ref, e_ref, gate_ref, rank_ref, cnt_ref, carry_ref, *, n_experts):
    i = pl.program_id(0)
    tm = lg_ref.shape[0]

    @pl.when(i == 0)
    def _():
        carry_ref[...] = jnp.zeros(carry_ref.shape, F32)

    lane = lax.broadcasted_iota(jnp.int32, (tm, LANES), 1)
    logits = jnp.where(lane < n_experts, lg_ref[...], -jnp.inf)
    vals, idxs, hots = [], [], []
    for _ in range(TOP_K):
        mx = jnp.max(logits, axis=-1, keepdims=True)
        idx = jnp.min(jnp.where(logits == mx, lane, LANES), axis=-1, keepdims=True)
        hot = lane == idx
        logits = jnp.where(hot, -jnp.inf, logits)
        vals.append(mx)
        idxs.append(idx)
        hots.append(hot)

    exps = [jnp.exp(v - vals[0]) for v in vals]
    denom = exps[0]
    for ex in exps[1:]:
        denom = denom + ex
    inv = 1.0 / denom

    hot_sum = jnp.zeros((tm, LANES), F32)
    for hot in hots:
        hot_sum = hot_sum + jnp.where(hot, 1.0, 0.0)
    r = lax.broadcasted_iota(jnp.int32, (tm, tm), 0)
    c = lax.broadcasted_iota(jnp.int32, (tm, tm), 1)
    earlier = jnp.where(c < r, 1.0, 0.0).astype(BF16)
    base = carry_ref[...] + jnp.dot(earlier, hot_sum.astype(BF16), preferred_element_type=F32)

    e_out = jnp.zeros((tm, LANES), jnp.int32)
    g_out = jnp.zeros((tm, LANES), F32)
    r_out = jnp.zeros((tm, LANES), jnp.int32)
    for k in range(TOP_K):
        rank = jnp.sum(jnp.where(hots[k], base, 0.0), axis=-1, keepdims=True).astype(jnp.int32)
        e_out = jnp.where(lane == k, idxs[k], e_out)
        g_out = jnp.where(lane == k, exps[k] * inv, g_out)
        r_out = jnp.where(lane == k, rank, r_out)
    e_ref[...] = e_out
    gate_ref[...] = g_out
    rank_ref[...] = r_out
    carry_ref[...] = carry_ref[...] + jnp.sum(hot_sum, axis=0, keepdims=True)
    cnt_ref[...] = carry_ref[...]


def _route(logits, n_experts):
    n = logits.shape[0]
    tm = ROUTE_TM
    row = lambda i: (i, 0)
    body = functools.partial(_route_body, n_experts=n_experts)
    return pl.pallas_call(
        body,
        out_shape=(jax.ShapeDtypeStruct((n, LANES), jnp.int32), jax.ShapeDtypeStruct((n, LANES), F32),
                   jax.ShapeDtypeStruct((n, LANES), jnp.int32), jax.ShapeDtypeStruct((1, LANES), F32)),
        grid=(n // tm,),
        in_specs=[pl.BlockSpec((tm, LANES), row)],
        out_specs=(pl.BlockSpec((tm, LANES), row), pl.BlockSpec((tm, LANES), row),
                   pl.BlockSpec((tm, LANES), row), pl.BlockSpec((1, LANES), lambda i: (0, 0))),
        scratch_shapes=[pltpu.VMEM((1, LANES), F32)],
        compiler_params=_cparams(("arbitrary",)),
        name="route",
    )(logits)


def _dispatch_body(ztile_ref, dest_ref, hn_ref, xs_ref, zeros, sem, zsem):
    tm = hn_ref.shape[0]

    @pl.when(pl.program_id(0) == 0)
    def _():
        zeros[...] = jnp.zeros(zeros.shape, zeros.dtype)

        def fill(z):
            row0 = pl.multiple_of(ztile_ref[z] * EXPERT_TM, EXPERT_TM)
            return pltpu.make_async_copy(zeros, xs_ref.at[pl.ds(row0, EXPERT_TM), :], zsem)

        def fill_start(z, carry):
            @pl.when(ztile_ref[z] >= 0)
            def _():
                fill(z).start()
            return carry

        def fill_wait(z, carry):
            @pl.when(ztile_ref[z] >= 0)
            def _():
                fill(z).wait()
            return carry

        lax.fori_loop(0, ztile_ref.shape[0], fill_start, 0)
        lax.fori_loop(0, ztile_ref.shape[0], fill_wait, 0)

    def start(t, carry):
        src = hn_ref.at[pl.ds(t, 1), :]
        for k in range(TOP_K):
            pltpu.make_async_copy(src, xs_ref.at[pl.ds(dest_ref[0, t * TOP_K + k], 1), :], sem).start()
        return carry

    lax.fori_loop(0, tm, start, 0)
    for _ in range(TOP_K):
        pltpu.make_async_copy(hn_ref, xs_ref.at[pl.ds(0, tm), :], sem).wait()


def _dispatch(zero_tiles, dest, hn, n_rows):
    n, d = hn.shape
    tm = DISPATCH_TM
    dest3 = dest.reshape(n // tm, 1, tm * TOP_K)
    return pl.pallas_call(
        _dispatch_body,
        out_shape=jax.ShapeDtypeStruct((n_rows, d), hn.dtype),
        grid_spec=pltpu.PrefetchScalarGridSpec(
            num_scalar_prefetch=1,
            grid=(n // tm,),
            in_specs=[
                pl.BlockSpec((None, 1, tm * TOP_K), lambda i, zt: (i, 0, 0), memory_space=pltpu.SMEM),
                pl.BlockSpec((tm, d), lambda i, zt: (i, 0)),
            ],
            out_specs=pl.BlockSpec(memory_space=pl.ANY),
            scratch_shapes=[pltpu.VMEM((EXPERT_TM, d), hn.dtype), pltpu.SemaphoreType.DMA,
                            pltpu.SemaphoreType.DMA],
        ),
        compiler_params=_cparams(("arbitrary",)),
        name="dispatch",
    )(zero_tiles, dest3, hn)


def _experts_body(te_ref, nu_ref, live_ref, x_ref, wg_ref, bg_ref, wu_ref, bu_ref, wd_ref, bd_ref, o_ref, xb_ref):
    i = pl.program_id(0)
    j = pl.program_id(1)
    tm = x_ref.shape[0]

    def compute(r):
        @pl.when(j == 0)
        def _():
            xb_ref[0:r, :] = x_ref[0:r, :].astype(BF16)
            o_ref[0:r, :] = jnp.broadcast_to(bd_ref[...], (r, o_ref.shape[1]))
            if r < tm:
                o_ref[r:tm, :] = jnp.zeros((tm - r, o_ref.shape[1]), F32)

        xb = xb_ref[0:r, :]
        g = jnp.dot(xb, wg_ref[...], preferred_element_type=F32) + bg_ref[...]
        u = jnp.dot(xb, wu_ref[...], preferred_element_type=F32) + bu_ref[...]
        g = jnp.minimum(g, SWIGLU_LIMIT)
        u = jnp.clip(u, -SWIGLU_LIMIT, SWIGLU_LIMIT)
        hdn = g * (1.0 / (1.0 + jnp.exp(-SWIGLU_ALPHA * g))) * (u + 1.0)
        o_ref[0:r, :] += jnp.dot(hdn.astype(BF16), wd_ref[...], preferred_element_type=F32)

    for r in range(EXPERT_RT, tm + 1, EXPERT_RT):
        pl.when(live_ref[i] == r)(functools.partial(compute, r))

    @pl.when((live_ref[i] == 0) & (j == 0))
    def _():
        o_ref[...] = jnp.zeros(o_ref.shape, F32)


def _experts(tile_expert, n_used, tile_live, xs, wg, bg, wu, bu, wd, bd):
    rows, d = xs.shape
    d_exp = wg.shape[2]
    tm, tc = EXPERT_TM, min(EXPERT_TC, d_exp)
    n_tiles = rows // tm
    nj = d_exp // tc

    def tile(i, nu):
        return jnp.minimum(i, nu[0] - 1)

    def chunk(i, j, nu):
        return jnp.where(i < nu[0], j, nj - 1)

    return pl.pallas_call(
        _experts_body,
        out_shape=jax.ShapeDtypeStruct((rows, d), F32),
        grid_spec=pltpu.PrefetchScalarGridSpec(
            num_scalar_prefetch=3,
            grid=(n_tiles, nj),
            in_specs=[
                pl.BlockSpec((tm, d), lambda i, j, te, nu, live: (tile(i, nu), 0)),
                pl.BlockSpec((None, d, tc), lambda i, j, te, nu, live: (te[i], 0, chunk(i, j, nu))),
                pl.BlockSpec((None, 1, tc), lambda i, j, te, nu, live: (te[i], 0, chunk(i, j, nu))),
                pl.BlockSpec((None, d, tc), lambda i, j, te, nu, live: (te[i], 0, chunk(i, j, nu))),
                pl.BlockSpec((None, 1, tc), lambda i, j, te, nu, live: (te[i], 0, chunk(i, j, nu))),
                pl.BlockSpec((None, tc, d), lambda i, j, te, nu, live: (te[i], chunk(i, j, nu), 0)),
                pl.BlockSpec((None, 1, d), lambda i, j, te, nu, live: (te[i], 0, 0)),
            ],
            out_specs=pl.BlockSpec((tm, d), lambda i, j, te, nu, live: (i, 0)),
            scratch_shapes=[pltpu.VMEM((tm, d), BF16)],
        ),
        compiler_params=_cparams(("arbitrary", "arbitrary"), BIG_VMEM_LIMIT),
        name="experts",
    )(tile_expert, n_used, tile_live, xs, wg, bg, wu, bu, wd, bd)


def _combine_body(dcur_ref, dnext_ref, gate_ref, h_ref, gfin_ref, ys_ref, o_ref, buf, sems, *, final_norm):
    i = pl.program_id(0)
    nb = pl.num_programs(0)
    tm = h_ref.shape[0]
    slot = i % 2

    def start_all(dref, s):
        def go(g, carry):
            t0 = pl.multiple_of(g * SUBLANES, SUBLANES)
            for r in range(SUBLANES):
                for k in range(TOP_K):
                    pltpu.make_async_copy(ys_ref.at[pl.ds(dref[0, (t0 + r) * TOP_K + k], 1), :],
                                          buf.at[s, k, pl.ds(t0 + r, 1), :], sems.at[s]).start()
            return carry
        lax.fori_loop(0, tm // SUBLANES, go, 0)

    @pl.when(i == 0)
    def _():
        start_all(dcur_ref, 0)

    @pl.when(i + 1 < nb)
    def _():
        start_all(dnext_ref, 1 - slot)

    for k in range(TOP_K):
        pltpu.make_async_copy(ys_ref.at[pl.ds(0, tm), :], buf.at[slot, k], sems.at[slot]).wait()

    y = h_ref[...]
    gates = gate_ref[...]
    for k in range(TOP_K):
        y = y + gates[:, k:k + 1] * buf[slot, k]
    if final_norm:
        ms = jnp.mean(y * y, axis=-1, keepdims=True)
        y = y * lax.rsqrt(ms + NORM_EPS) * gfin_ref[...]
    o_ref[...] = y


def _combine(dest, gates, h, g_final, ys, *, final_norm):
    n, d = h.shape
    tm = COMBINE_TM
    nb = n // tm
    dest3 = dest.reshape(nb, 1, tm * TOP_K)
    smem_blk = (None, 1, tm * TOP_K)
    return pl.pallas_call(
        functools.partial(_combine_body, final_norm=final_norm),
        out_shape=jax.ShapeDtypeStruct((n, d), F32),
        grid=(nb,),
        in_specs=[
            pl.BlockSpec(smem_blk, lambda i: (i, 0, 0), memory_space=pltpu.SMEM),
            pl.BlockSpec(smem_blk, lambda i: (jnp.minimum(i + 1, nb - 1), 0, 0), memory_space=pltpu.SMEM),
            pl.BlockSpec((tm, LANES), lambda i: (i, 0)),
            pl.BlockSpec((tm, d), lambda i: (i, 0)),
            pl.BlockSpec((1, d), lambda i: (0, 0)),
            pl.BlockSpec(memory_space=pl.ANY),
        ],
        out_specs=pl.BlockSpec((tm, d), lambda i: (i, 0)),
        scratch_shapes=[pltpu.VMEM((2, TOP_K, tm, d), F32), pltpu.SemaphoreType.DMA((2,))],
        compiler_params=_cparams(("arbitrary",)),
        name="combine",
    )(dest3, dest3, gates, h, g_final, ys)


def _rope_tables(positions):
    inv_freq = ROPE_THETA ** (-jnp.arange(0, HEAD_DIM, 2, dtype=F32) / HEAD_DIM)
    ang = positions.astype(F32)[..., None] * inv_freq
    cos, sin = jnp.cos(ang), jnp.sin(ang)
    return jnp.concatenate([cos, cos], axis=-1), jnp.concatenate([-sin, sin], axis=-1)


def _layer(x2, cos, sin_signed, layer_idx, bsz, seq, g_attn_norm, w_in, b_forget, lambda_q1, lambda_k1,
           lambda_q2, lambda_k2, g_diff_sub, g_fox_out, w_out, g_ffn_norm, w_router, b_router,
           w_gate, b_gate, w_up, b_up, w_down, b_down):
    n, d = x2.shape
    diff_w = d // 2
    fox_w = d - diff_w
    fox_heads = fox_w // HEAD_DIM
    main_cols = 3 * diff_w + 3 * fox_w
    n_experts = w_router.shape[1]
    lam_init = _lambda_init(layer_idx)

    d_exp = w_gate.shape[2]
    hosted = {"diff": (w_down.reshape(n_experts * d_exp, d), w_up.reshape(n_experts * d, d_exp)),
              "fox": (w_gate.reshape(n_experts * d, d_exp),)}
    host_steps = {"diff": bsz * (diff_w // (2 * HEAD_DIM)) * (seq // min(DIFF_T, seq)),
                  "fox": bsz * fox_heads * (seq // min(FOX_T, seq))}
    fused = {k: _can_cast_rows(ws, host_steps[k]) for k, ws in hosted.items()}

    w_main_t = jnp.swapaxes(w_in, 0, 1)[:main_cols].astype(BF16)
    w_f_hi, w_f_lo = _split_bf16(jnp.pad(w_in[:, main_cols:], ((0, 0), (0, LANES - fox_heads))))
    proj, f_logit = _inproj(x2, g_attn_norm[None, :], w_main_t, w_f_hi, w_f_lo, cos, sin_signed, (),
                            diff_w=diff_w, fox_w=fox_w)

    z_rows = f_logit[:, :fox_heads].reshape(bsz, seq, fox_heads).transpose(0, 2, 1).reshape(bsz * fox_heads, seq)
    b_rows = jnp.tile(b_forget.astype(F32), bsz)[:, None]
    c_rows = _cumgate(z_rows, b_rows)

    od, *diff_cast = _diff_attention(proj, lambda_q1[None, :], lambda_k1[None, :], lambda_q2[None, :],
                                     lambda_k2[None, :], g_diff_sub[None, :], hosted["diff"] if fused["diff"] else (),
                                     bsz=bsz, seq=seq, diff_w=diff_w, lam_init=lam_init)
    c_keys = (c_rows * LOG2E).reshape(bsz * fox_heads, seq // LANES, LANES)
    of, *fox_cast = _fox_attention(proj, c_keys, g_fox_out[None, :], hosted["fox"] if fused["fox"] else (),
                                   bsz=bsz, seq=seq, diff_w=diff_w, fox_w=fox_w)
    w_down_b, w_up_b = diff_cast if fused["diff"] else (w.astype(BF16) for w in hosted["diff"])
    (w_gate_b,) = fox_cast if fused["fox"] else (w.astype(BF16) for w in hosted["fox"])
    w_down_b, w_up_b, w_gate_b = (w_down_b.reshape(w_down.shape), w_up_b.reshape(w_up.shape),
                                  w_gate_b.reshape(w_gate.shape))

    w_out_b = w_out.astype(BF16)
    w_router_hi, w_router_lo = _split_bf16(jnp.pad(w_router, ((0, 0), (0, LANES - n_experts))))
    b_router_p = jnp.pad(b_router, (0, LANES - n_experts))[None, :]
    h, hn, logits = _outproj(od, of, w_out_b[:diff_w], w_out_b[diff_w:], x2, g_ffn_norm[None, :],
                             w_router_hi, w_router_lo, b_router_p)

    e_pad, gate_pad, rank_pad, cnt = _route(logits, n_experts)
    e_idx = e_pad[:, :TOP_K]
    counts = cnt[0, :n_experts].astype(jnp.int32)
    padded = (counts + EXPERT_TM - 1) // EXPERT_TM * EXPERT_TM
    pad_ends = jnp.cumsum(padded)
    pad_starts = pad_ends - padded
    start_of = jnp.sum(jnp.where(e_idx[..., None] == jnp.arange(n_experts), pad_starts, 0), axis=-1)
    dest = (start_of + rank_pad[:, :TOP_K]).astype(jnp.int32)
    n_tiles = -(-n * TOP_K // EXPERT_TM) + n_experts
    tile_start = jnp.arange(n_tiles, dtype=jnp.int32) * EXPERT_TM
    tile_expert = jnp.sum((pad_ends[None, :] <= tile_start[:, None]).astype(jnp.int32), axis=1)
    tile_expert = jnp.minimum(tile_expert, n_experts - 1)
    n_used = (pad_ends[-1:] // EXPERT_TM).astype(jnp.int32)
    tile_expert = jnp.where(jnp.arange(n_tiles) < n_used[0], tile_expert, tile_expert[n_used[0] - 1])
    last_tile = jnp.where(padded > 0, pad_ends // EXPERT_TM - 1, -1)
    tail_tile = n_used[0] + jnp.arange(n_tiles - n * TOP_K // EXPERT_TM, dtype=jnp.int32)
    tail_tile = jnp.where(tail_tile < n_tiles, tail_tile, -1)
    zero_tiles = jnp.concatenate([last_tile, tail_tile]).astype(jnp.int32)

    routed_end = (pad_starts + counts)[tile_expert]
    tile_live = jnp.clip(routed_end - tile_start, 0, EXPERT_TM)
    tile_live = jnp.where(jnp.arange(n_tiles) < n_used[0], tile_live, 0)
    tile_live = ((tile_live + EXPERT_RT - 1) // EXPERT_RT * EXPERT_RT).astype(jnp.int32)

    xs = _dispatch(zero_tiles, dest, hn, n_tiles * EXPERT_TM)
    ys = _experts(tile_expert, n_used, tile_live, xs, w_gate_b, b_gate[:, None, :], w_up_b, b_up[:, None, :],
                  w_down_b, b_down[:, None, :])
    return dest, gate_pad, h, ys


def kernel(x, positions, g_attn_norm, w_in, b_forget, lambda_q1, lambda_k1, lambda_q2, lambda_k2, g_diff_sub,
           g_fox_out, w_out, g_ffn_norm, w_router, b_router, w_gate, b_gate, w_up, b_up, w_down, b_down,
           g_final):
    bsz, seq, d = x.shape
    depth = w_in.shape[0]
    cos, sin_signed = _rope_tables(positions)
    cos = cos.reshape(bsz * seq, HEAD_DIM)
    sin_signed = sin_signed.reshape(bsz * seq, HEAD_DIM)
    x2 = x.reshape(bsz * seq, d)
    for l in range(depth):
        dest, gates, h, ys = _layer(
            x2, cos, sin_signed, l, bsz, seq, g_attn_norm[l], w_in[l], b_forget[l], lambda_q1[l], lambda_k1[l],
            lambda_q2[l], lambda_k2[l], g_diff_sub[l], g_fox_out[l], w_out[l], g_ffn_norm[l], w_router[l],
            b_router[l], w_gate[l], b_gate[l], w_up[l], b_up[l], w_down[l], b_down[l])
        x2 = _combine(dest, gates, h, g_final[None, :], ys, final_norm=(l == depth - 1))
    return x2.reshape(bsz, seq, d)
```

```python
import functools
import math

import jax
import jax.numpy as jnp
from jax import lax
from jax.experimental import pallas as pl
from jax.experimental.pallas import tpu as pltpu

F32 = jnp.float32
BF16 = jnp.bfloat16

HEAD_DIM = 128
CHUNK = 64
ROPE_THETA = 10000.0
NORM_EPS = 1e-5
NEG_BIG = -1e30
TOP_K = 4
SWIGLU_LIMIT = 7.0
SWIGLU_ALPHA = 1.702
LANES = 128
SUBLANES = 8
QK_SCALE = HEAD_DIM ** -0.5
LOG2E = math.log2(math.e)
ONES_ROWS = 16

INPROJ_TM = 1024
INPROJ_TN = 512
DIFF_T = 512
FOX_T = 1024
CAST_ROWS = 512
VT_CHUNK = 512
OUTPROJ_TM = 512
ROUTE_TM = 512
DISPATCH_TM = 256
EXPERT_TM = 512
EXPERT_RT = 128
EXPERT_TC = 1024
COMBINE_TM = 128
VMEM_LIMIT = 48 * 1024 * 1024
BIG_VMEM_LIMIT = 58 * 1024 * 1024


def _lambda_init(layer_idx):
    return 0.8 - 0.6 * math.exp(-0.3 * layer_idx)


def _cparams(sem, vmem_limit=VMEM_LIMIT):
    return pltpu.CompilerParams(dimension_semantics=sem, vmem_limit_bytes=vmem_limit)


def _split_bf16(x):
    hi = x.astype(BF16)
    return hi, (x - hi.astype(F32)).astype(BF16)


def _dot_nt(a, b):
    return lax.dot_general(a, b, (((1,), (1,)), ((), ())), preferred_element_type=F32)


def _dot3(x_hi, x_lo, w_hi, w_lo, dot=functools.partial(jnp.dot, preferred_element_type=F32)):
    return dot(x_hi, w_hi) + (dot(x_hi, w_lo) + dot(x_lo, w_hi))


def _cast_rows_specs(weights, step_of):
    in_specs, out_specs, out_shapes = [], [], []
    for w in weights:
        rows, cols = w.shape
        last = rows // CAST_ROWS - 1
        spec = pl.BlockSpec((CAST_ROWS, cols), lambda *idx, last=last: (jnp.minimum(step_of(*idx), last), 0))
        in_specs.append(spec)
        out_specs.append(spec)
        out_shapes.append(jax.ShapeDtypeStruct((rows, cols), BF16))
    return in_specs, out_specs, out_shapes


def _can_cast_rows(weights, steps):
    return all(w.shape[0] % CAST_ROWS == 0 and w.shape[0] // CAST_ROWS <= steps for w in weights)


def _linear_step(n_axes):
    step = pl.program_id(0)
    for axis in range(1, n_axes):
        step = step * pl.num_programs(axis) + pl.program_id(axis)
    return step


def _cast_rows(step, in_refs, out_refs, total_rows):
    for src, dst, rows in zip(in_refs, out_refs, total_rows):
        @pl.when(step < rows // CAST_ROWS)
        def _():
            dst[...] = src[...].astype(BF16)


def _inproj_body(x_ref, g_ref, w_ref, wfh_ref, wfl_ref, cos_ref, sin_ref, *rest, n_rope, tn, q_ranges, cast_rows):
    n_cast = len(cast_rows)
    cast_in, (o_ref, fl_ref, *cast_out), (xn_ref,) = rest[:n_cast], rest[n_cast:2 * n_cast + 2], rest[2 * n_cast + 2:]
    _cast_rows(_linear_step(2), cast_in, cast_out, cast_rows)
    j = pl.program_id(1)

    @pl.when(j == 0)
    def _():
        x = x_ref[...]
        ms = jnp.mean(x * x, axis=-1, keepdims=True)
        xn = x * lax.rsqrt(ms + NORM_EPS) * g_ref[...]
        xn_hi, xn_lo = _split_bf16(xn)
        xn_ref[...] = xn_hi
        fl_ref[...] = _dot3(xn_hi, xn_lo, wfh_ref[...], wfl_ref[...], dot=_dot_nt)

    acc = _dot_nt(xn_ref[...], w_ref[...])
    col0 = j * tn
    is_q = (col0 >= q_ranges[0][0]) & (col0 < q_ranges[0][1])
    is_q = is_q | ((col0 >= q_ranges[1][0]) & (col0 < q_ranges[1][1]))
    scale = jnp.where(is_q, QK_SCALE * LOG2E, 1.0).astype(F32)

    @pl.when(j < n_rope)
    def _():
        cos = cos_ref[...] * scale
        sin = sin_ref[...] * scale
        for gi in range(tn // LANES):
            xg = acc[:, gi * LANES:(gi + 1) * LANES]
            rot = pltpu.roll(xg, HEAD_DIM // 2, 1)
            o_ref[:, gi * LANES:(gi + 1) * LANES] = (xg * cos + rot * sin).astype(BF16)

    @pl.when(j >= n_rope)
    def _():
        o_ref[...] = (acc * scale).astype(BF16)


def _inproj(x2, g, w_main_t, w_f_hi, w_f_lo, cos, sin_signed, cast_weights, *, diff_w, fox_w):
    n, d = x2.shape
    cols = w_main_t.shape[0]
    tm, tn = min(INPROJ_TM, n), INPROJ_TN
    n_rope = 2 * diff_w // tn
    q_ranges = ((0, diff_w), (3 * diff_w, 3 * diff_w + fox_w))
    cast_in, cast_out, cast_shapes = _cast_rows_specs(cast_weights, lambda i, j: i * (cols // tn) + j)
    body = functools.partial(_inproj_body, n_rope=n_rope, tn=tn, q_ranges=q_ranges,
                             cast_rows=tuple(w.shape[0] for w in cast_weights))
    return pl.pallas_call(
        body,
        out_shape=(jax.ShapeDtypeStruct((n, cols), BF16), jax.ShapeDtypeStruct((n, LANES), F32), *cast_shapes),
        grid=(n // tm, cols // tn),
        in_specs=[
            pl.BlockSpec((tm, d), lambda i, j: (i, 0)),
            pl.BlockSpec((1, d), lambda i, j: (0, 0)),
            pl.BlockSpec((tn, d), lambda i, j: (j, 0)),
            pl.BlockSpec((LANES, d), lambda i, j: (0, 0)),
            pl.BlockSpec((LANES, d), lambda i, j: (0, 0)),
            pl.BlockSpec((tm, LANES), lambda i, j: (i, 0)),
            pl.BlockSpec((tm, LANES), lambda i, j: (i, 0)),
            *cast_in,
        ],
        out_specs=(pl.BlockSpec((tm, tn), lambda i, j: (i, j)),
                   pl.BlockSpec((tm, LANES), lambda i, j: (i, 0)), *cast_out),
        scratch_shapes=[pltpu.VMEM((tm, d), BF16)],
        compiler_params=_cparams(("arbitrary", "arbitrary"), BIG_VMEM_LIMIT),
        name="inproj",
    )(x2, g, w_main_t, w_f_hi, w_f_lo, cos, sin_signed, *cast_weights)


def _cumgate_body(z_ref, b_ref, c_ref):
    z = z_ref[...] + b_ref[...]
    v = jnp.minimum(z, 0.0) - jnp.log(1.0 + jnp.exp(-jnp.abs(z)))
    seq = v.shape[1]
    lane = lax.broadcasted_iota(jnp.int32, v.shape, 1)
    shift = 1
    while shift < seq:
        v = v + jnp.where(lane >= shift, pltpu.roll(v, shift, 1), 0.0)
        shift *= 2
    c_ref[...] = v


def _cumgate(z_rows, b_rows):
    return pl.pallas_call(
        _cumgate_body,
        out_shape=jax.ShapeDtypeStruct(z_rows.shape, F32),
        name="cumgate",
    )(z_rows, b_rows)


def _build_vt(v_ref, vt_ref):
    seq, dv = v_ref.shape
    for c in range(seq // VT_CHUNK):
        blk = v_ref[c * VT_CHUNK:(c + 1) * VT_CHUNK, :].astype(F32)
        vt_ref[0:dv, c * VT_CHUNK:(c + 1) * VT_CHUNK] = blk.T.astype(BF16)
    vt_ref[dv:dv + ONES_ROWS, :] = jnp.ones((ONES_ROWS, seq), BF16)


def _softmax_step(s_t, vt_blk, m_ref, acc_ref):
    m_prev = m_ref[...]
    m_new = jnp.maximum(m_prev, jnp.max(s_t, axis=0, keepdims=True))
    alpha = jnp.exp2(m_prev - m_new)
    p_t = jnp.exp2(s_t - m_new).astype(BF16)
    acc_ref[...] = alpha * acc_ref[...] + jnp.dot(vt_blk, p_t, preferred_element_type=F32)
    m_ref[...] = m_new


def _sweep_keys(qi, t, step):
    def pair(i, carry):
        step(pl.multiple_of(i * (2 * t), 2 * t), 2 * t, False)
        return carry

    lax.fori_loop(0, qi // 2, pair, 0)

    @pl.when(qi % 2 == 1)
    def _():
        step(pl.multiple_of((qi - 1) * t, t), t, False)

    step(pl.multiple_of(qi * t, t), t, True)


def _normalised(acc, dv):
    return acc[0:dv] * (1.0 / acc[dv:dv + 1])


def _diff_body(lq1_ref, lk1_ref, lq2_ref, lk2_ref, gsub_ref, q1_ref, q2_ref, k1_ref, k2_ref, v_ref,
               *rest, t, lam_init, cast_rows):
    n_cast = len(cast_rows)
    cast_in, (o_ref, *cast_out), (vt, m1, a1, m2, a2) = rest[:n_cast], rest[n_cast:2 * n_cast + 1], rest[2 * n_cast + 1:]
    _cast_rows(_linear_step(3), cast_in, cast_out, cast_rows)
    qi = pl.program_id(2)
    dv = v_ref.shape[1]

    @pl.when(qi == 0)
    def _():
        _build_vt(v_ref, vt)

    for m_ref, a_ref in ((m1, a1), (m2, a2)):
        m_ref[...] = jnp.full(m_ref.shape, NEG_BIG, F32)
        a_ref[...] = jnp.zeros(a_ref.shape, F32)
    q1 = q1_ref[...]
    q2 = q2_ref[...]
    key_chunk = lax.broadcasted_iota(jnp.int32, (t, t), 0) // CHUNK
    qry_chunk = lax.broadcasted_iota(jnp.int32, (t, t), 1) // CHUNK
    visible = key_chunk <= qry_chunk

    def step(off, size, diagonal):
        vt_blk = vt[:, pl.ds(off, size)]
        for q, k_ref, m_ref, a_ref in ((q1, k1_ref, m1, a1), (q2, k2_ref, m2, a2)):
            s_t = _dot_nt(k_ref[pl.ds(off, size), :], q)
            if diagonal:
                s_t = jnp.where(visible, s_t, NEG_BIG)
            _softmax_step(s_t, vt_blk, m_ref, a_ref)

    _sweep_keys(qi, t, step)

    lam = (jnp.exp(jnp.sum(lq1_ref[...] * lk1_ref[...], axis=-1, keepdims=True))
           - jnp.exp(jnp.sum(lq2_ref[...] * lk2_ref[...], axis=-1, keepdims=True)) + lam_init)
    o = (_normalised(a1[...], dv) - lam * _normalised(a2[...], dv)).T
    ms = jnp.mean(o * o, axis=-1, keepdims=True)
    y = o * lax.rsqrt(ms + NORM_EPS) * gsub_ref[...]
    o_ref[...] = (y * (1.0 - lam_init)).astype(BF16)


def _diff_attention(proj, lq1, lk1, lq2, lk2, gsub, cast_weights, *, bsz, seq, diff_w, lam_init):
    t = min(DIFF_T, seq)
    dv = 2 * HEAD_DIM
    n_heads = diff_w // dv
    nq = seq // t
    kcol = diff_w // HEAD_DIM
    vcol = 2 * diff_w // dv
    vec = pl.BlockSpec((1, HEAD_DIM), lambda b, h, i: (0, 0))
    cast_in, cast_out, cast_shapes = _cast_rows_specs(cast_weights, lambda b, h, i: (b * n_heads + h) * nq + i)
    body = functools.partial(_diff_body, t=t, lam_init=lam_init, cast_rows=tuple(w.shape[0] for w in cast_weights))
    return pl.pallas_call(
        body,
        out_shape=(jax.ShapeDtypeStruct((bsz * seq, diff_w), BF16), *cast_shapes),
        grid=(bsz, n_heads, nq),
        in_specs=[
            vec, vec, vec, vec,
            pl.BlockSpec((1, dv), lambda b, h, i: (0, 0)),
            pl.BlockSpec((t, HEAD_DIM), lambda b, h, i: (b * nq + i, 2 * h)),
            pl.BlockSpec((t, HEAD_DIM), lambda b, h, i: (b * nq + i, 2 * h + 1)),
            pl.BlockSpec((seq, HEAD_DIM), lambda b, h, i: (b, kcol + 2 * h)),
            pl.BlockSpec((seq, HEAD_DIM), lambda b, h, i: (b, kcol + 2 * h + 1)),
            pl.BlockSpec((seq, dv), lambda b, h, i: (b, vcol + h)),
            *cast_in,
        ],
        out_specs=(pl.BlockSpec((t, dv), lambda b, h, i: (b * nq + i, h)), *cast_out),
        scratch_shapes=[pltpu.VMEM((dv + ONES_ROWS, seq), BF16),
                        pltpu.VMEM((1, t), F32), pltpu.VMEM((dv + ONES_ROWS, t), F32),
                        pltpu.VMEM((1, t), F32), pltpu.VMEM((dv + ONES_ROWS, t), F32)],
        compiler_params=_cparams(("arbitrary", "arbitrary", "arbitrary"), BIG_VMEM_LIMIT),
        name="diffattn",
    )(lq1, lk1, lq2, lk2, gsub, proj, proj, proj, proj, proj, *cast_weights)


def _fox_body(g_ref, q_ref, k_ref, ck_ref, v_ref, *rest, t, cast_rows):
    n_cast = len(cast_rows)
    cast_in, (o_ref, *cast_out), (vt, ck_lanes, m, acc) = rest[:n_cast], rest[n_cast:2 * n_cast + 1], rest[2 * n_cast + 1:]
    _cast_rows(_linear_step(3), cast_in, cast_out, cast_rows)
    qi = pl.program_id(2)
    dv = v_ref.shape[1]

    @pl.when(qi == 0)
    def _():
        _build_vt(v_ref, vt)
        for j in range(ck_ref.shape[0]):
            ck_lanes[j * LANES:(j + 1) * LANES, :] = jnp.broadcast_to(ck_ref[j:j + 1, :], (LANES, LANES)).T

    m[...] = jnp.full(m.shape, NEG_BIG, F32)
    acc[...] = jnp.zeros(acc.shape, F32)
    q = q_ref[...]
    causal = lax.broadcasted_iota(jnp.int32, (t, t), 0) <= lax.broadcasted_iota(jnp.int32, (t, t), 1)

    def step(off, size, diagonal):
        ck = ck_lanes[pl.ds(off, size), :]
        s_t = _dot_nt(k_ref[pl.ds(off, size), :], q) - jnp.concatenate([ck] * (t // LANES), axis=1)
        if diagonal:
            s_t = jnp.where(causal, s_t, NEG_BIG)
        _softmax_step(s_t, vt[:, pl.ds(off, size)], m, acc)

    _sweep_keys(qi, t, step)

    o = _normalised(acc[...], dv).T
    ms = jnp.mean(o * o, axis=-1, keepdims=True)
    o_ref[...] = (o * lax.rsqrt(ms + NORM_EPS) * g_ref[...]).astype(BF16)


def _fox_attention(proj, c_keys, g_fox, cast_weights, *, bsz, seq, diff_w, fox_w):
    t = min(FOX_T, seq)
    n_heads = fox_w // HEAD_DIM
    nq = seq // t
    qcol = 3 * diff_w // HEAD_DIM
    kcol = qcol + n_heads
    vcol = kcol + n_heads
    cast_in, cast_out, cast_shapes = _cast_rows_specs(cast_weights, lambda b, h, i: (b * n_heads + h) * nq + i)
    body = functools.partial(_fox_body, t=t, cast_rows=tuple(w.shape[0] for w in cast_weights))
    return pl.pallas_call(
        body,
        out_shape=(jax.ShapeDtypeStruct((bsz * seq, fox_w), BF16), *cast_shapes),
        grid=(bsz, n_heads, nq),
        in_specs=[
            pl.BlockSpec((1, HEAD_DIM), lambda b, h, i: (0, 0)),
            pl.BlockSpec((t, HEAD_DIM), lambda b, h, i: (b * nq + i, qcol + h)),
            pl.BlockSpec((seq, HEAD_DIM), lambda b, h, i: (b, kcol + h)),
            pl.BlockSpec((None, seq // LANES, LANES), lambda b, h, i: (b * n_heads + h, 0, 0)),
            pl.BlockSpec((seq, HEAD_DIM), lambda b, h, i: (b, vcol + h)),
            *cast_in,
        ],
        out_specs=(pl.BlockSpec((t, HEAD_DIM), lambda b, h, i: (b * nq + i, h)), *cast_out),
        scratch_shapes=[pltpu.VMEM((HEAD_DIM + ONES_ROWS, seq), BF16), pltpu.VMEM((seq, LANES), F32),
                        pltpu.VMEM((1, t), F32), pltpu.VMEM((HEAD_DIM + ONES_ROWS, t), F32)],
        compiler_params=_cparams(("arbitrary", "arbitrary", "arbitrary"), BIG_VMEM_LIMIT),
        name="foxattn",
    )(g_fox, proj, proj, c_keys, proj, *cast_weights)


def _outproj_body(od_ref, of_ref, wd_ref, wf_ref, x_ref, g_ref, wrh_ref, wrl_ref, br_ref, h_ref, hn_ref, lg_ref):
    h = (x_ref[...]
         + jnp.dot(od_ref[...], wd_ref[...], preferred_element_type=F32)
         + jnp.dot(of_ref[...], wf_ref[...], preferred_element_type=F32))
    h_ref[...] = h
    ms = jnp.mean(h * h, axis=-1, keepdims=True)
    hn = h * lax.rsqrt(ms + NORM_EPS) * g_ref[...]
    hn_ref[...] = hn
    hn_hi, hn_lo = _split_bf16(hn)
    lg_ref[...] = _dot3(hn_hi, hn_lo, wrh_ref[...], wrl_ref[...]) + br_ref[...]


def _outproj(od, of, w_od, w_of, x2, g_ffn, w_router_hi, w_router_lo, b_router_p):
    n, d = x2.shape
    tm = OUTPROJ_TM
    row = lambda i: (i, 0)
    const = lambda i: (0, 0)
    resident = functools.partial(pl.BlockSpec, index_map=const, pipeline_mode=pl.Buffered(1))
    return pl.pallas_call(
        _outproj_body,
        out_shape=(jax.ShapeDtypeStruct((n, d), F32), jax.ShapeDtypeStruct((n, d), F32),
                   jax.ShapeDtypeStruct((n, LANES), F32)),
        grid=(n // tm,),
        in_specs=[
            pl.BlockSpec((tm, od.shape[1]), row),
            pl.BlockSpec((tm, of.shape[1]), row),
            resident(w_od.shape),
            resident(w_of.shape),
            pl.BlockSpec((tm, d), row),
            pl.BlockSpec((1, d), const),
            resident((d, LANES)),
            resident((d, LANES)),
            pl.BlockSpec((1, LANES), const),
        ],
        out_specs=(pl.BlockSpec((tm, d), row), pl.BlockSpec((tm, d), row), pl.BlockSpec((tm, LANES), row)),
        compiler_params=_cparams(("parallel",)),
        name="outproj",
    )(od, of, w_od, w_of, x2, g_ffn, w_router_hi, w_router_lo, b_router_p)


def _route_body(lg_ref, e_ref, gate_ref, rank_ref, cnt_ref, carry_ref, *, n_experts):
    i = pl.program_id(0)
    tm = lg_ref.shape[0]

    @pl.when(i == 0)
    def _():
        carry_ref[...] = jnp.zeros(carry_ref.shape, F32)

    lane = lax.broadcasted_iota(jnp.int32, (tm, LANES), 1)
    logits = jnp.where(lane < n_experts, lg_ref[...], -jnp.inf)
    vals, idxs, hots = [], [], []
    for _ in range(TOP_K):
        mx = jnp.max(logits, axis=-1, keepdims=True)
        idx = jnp.min(jnp.where(logits == mx, lane, LANES), axis=-1, keepdims=True)
        hot = lane == idx
        logits = jnp.where(hot, -jnp.inf, logits)
        vals.append(mx)
        idxs.append(idx)
        hots.append(hot)

    exps = [jnp.exp(v - vals[0]) for v in vals]
    denom = exps[0]
    for ex in exps[1:]:
        denom = denom + ex
    inv = 1.0 / denom

    hot_sum = jnp.zeros((tm, LANES), F32)
    for hot in hots:
        hot_sum = hot_sum + jnp.where(hot, 1.0, 0.0)
    r = lax.broadcasted_iota(jnp.int32, (tm, tm), 0)
    c = lax.broadcasted_iota(jnp.int32, (tm, tm), 1)
    earlier = jnp.where(c < r, 1.0, 0.0).astype(BF16)
    base = carry_ref[...] + jnp.dot(earlier, hot_sum.astype(BF16), preferred_element_type=F32)

    e_out = jnp.zeros((tm, LANES), jnp.int32)
    g_out = jnp.zeros((tm, LANES), F32)
    r_out = jnp.zeros((tm, LANES), jnp.int32)
    for k in range(TOP_K):
        rank = jnp.sum(jnp.where(hots[k], base, 0.0), axis=-1, keepdims=True).astype(jnp.int32)
        e_out = jnp.where(lane == k, idxs[k], e_out)
        g_out = jnp.where(lane == k, exps[k] * inv, g_out)
        r_out = jnp.where(lane == k, rank, r_out)
    e_ref[...] = e_out
    gate_ref[...] = g_out
    rank_ref[...] = r_out
    carry_ref[...] = carry_ref[...] + jnp.sum(hot_sum, axis=0, keepdims=True)
    cnt_ref[...] = carry_ref[...]


def _route(logits, n_experts):
    n = logits.shape[0]
    tm = ROUTE_TM
    row = lambda i: (i, 0)
    body = functools.partial(_route_body, n_experts=n_experts)
    return pl.pallas_call(
        body,
        out_shape=(jax.ShapeDtypeStruct((n, LANES), jnp.int32), jax.ShapeDtypeStruct((n, LANES), F32),
                   jax.ShapeDtypeStruct((n, LANES), jnp.int32), jax.ShapeDtypeStruct((1, LANES), F32)),
        grid=(n // tm,),
        in_specs=[pl.BlockSpec((tm, LANES), row)],
        out_specs=(pl.BlockSpec((tm, LANES), row), pl.BlockSpec((tm, LANES), row),
                   pl.BlockSpec((tm, LANES), row), pl.BlockSpec((1, LANES), lambda i: (0, 0))),
        scratch_shapes=[pltpu.VMEM((1, LANES), F32)],
        compiler_params=_cparams(("arbitrary",)),
        name="route",
    )(logits)


def _dispatch_body(ztile_ref, dest_ref, hn_ref, xs_ref, zeros, sem, zsem):
    tm = hn_ref.shape[0]

    @pl.when(pl.program_id(0) == 0)
    def _():
        zeros[...] = jnp.zeros(zeros.shape, zeros.dtype)

        def fill(z):
            row0 = pl.multiple_of(ztile_ref[z] * EXPERT_TM, EXPERT_TM)
            return pltpu.make_async_copy(zeros, xs_ref.at[pl.ds(row0, EXPERT_TM), :], zsem)

        def fill_start(z, carry):
            @pl.when(ztile_ref[z] >= 0)
            def _():
                fill(z).start()
            return carry

        def fill_wait(z, carry):
            @pl.when(ztile_ref[z] >= 0)
            def _():
                fill(z).wait()
            return carry

        lax.fori_loop(0, ztile_ref.shape[0], fill_start, 0)
        lax.fori_loop(0, ztile_ref.shape[0], fill_wait, 0)

    def start(t, carry):
        src = hn_ref.at[pl.ds(t, 1), :]
        for k in range(TOP_K):
            pltpu.make_async_copy(src, xs_ref.at[pl.ds(dest_ref[0, t * TOP_K + k], 1), :], sem).start()
        return carry

    lax.fori_loop(0, tm, start, 0)
    for _ in range(TOP_K):
        pltpu.make_async_copy(hn_ref, xs_ref.at[pl.ds(0, tm), :], sem).wait()


def _dispatch(zero_tiles, dest, hn, n_rows):
    n, d = hn.shape
    tm = DISPATCH_TM
    dest3 = dest.reshape(n // tm, 1, tm * TOP_K)
    return pl.pallas_call(
        _dispatch_body,
        out_shape=jax.ShapeDtypeStruct((n_rows, d), hn.dtype),
        grid_spec=pltpu.PrefetchScalarGridSpec(
            num_scalar_prefetch=1,
            grid=(n // tm,),
            in_specs=[
                pl.BlockSpec((None, 1, tm * TOP_K), lambda i, zt: (i, 0, 0), memory_space=pltpu.SMEM),
                pl.BlockSpec((tm, d), lambda i, zt: (i, 0)),
            ],
            out_specs=pl.BlockSpec(memory_space=pl.ANY),
            scratch_shapes=[pltpu.VMEM((EXPERT_TM, d), hn.dtype), pltpu.SemaphoreType.DMA,
                            pltpu.SemaphoreType.DMA],
        ),
        compiler_params=_cparams(("arbitrary",)),
        name="dispatch",
    )(zero_tiles, dest3, hn)


def _experts_body(te_ref, nu_ref, live_ref, x_ref, wg_ref, bg_ref, wu_ref, bu_ref, wd_ref, bd_ref, o_ref, xb_ref):
    i = pl.program_id(0)
    j = pl.program_id(1)
    tm = x_ref.shape[0]

    def compute(r):
        @pl.when(j == 0)
        def _():
            xb_ref[0:r, :] = x_ref[0:r, :].astype(BF16)
            o_ref[0:r, :] = jnp.broadcast_to(bd_ref[...], (r, o_ref.shape[1]))
            if r < tm:
                o_ref[r:tm, :] = jnp.zeros((tm - r, o_ref.shape[1]), F32)

        xb = xb_ref[0:r, :]
        g = jnp.dot(xb, wg_ref[...], preferred_element_type=F32) + bg_ref[...]
        u = jnp.dot(xb, wu_ref[...], preferred_element_type=F32) + bu_ref[...]
        g = jnp.minimum(g, SWIGLU_LIMIT)
        u = jnp.clip(u, -SWIGLU_LIMIT, SWIGLU_LIMIT)
        hdn = g * (1.0 / (1.0 + jnp.exp(-SWIGLU_ALPHA * g))) * (u + 1.0)
        o_ref[0:r, :] += jnp.dot(hdn.astype(BF16), wd_ref[...], preferred_element_type=F32)

    for r in range(EXPERT_RT, tm + 1, EXPERT_RT):
        pl.when(live_ref[i] == r)(functools.partial(compute, r))

    @pl.when((live_ref[i] == 0) & (j == 0))
    def _():
        o_ref[...] = jnp.zeros(o_ref.shape, F32)


def _experts(tile_expert, n_used, tile_live, xs, wg, bg, wu, bu, wd, bd):
    rows, d = xs.shape
    d_exp = wg.shape[2]
    tm, tc = EXPERT_TM, min(EXPERT_TC, d_exp)
    n_tiles = rows // tm
    nj = d_exp // tc

    def tile(i, nu):
        return jnp.minimum(i, nu[0] - 1)

    def chunk(i, j, nu):
        return jnp.where(i < nu[0], j, nj - 1)

    return pl.pallas_call(
        _experts_body,
        out_shape=jax.ShapeDtypeStruct((rows, d), F32),
        grid_spec=pltpu.PrefetchScalarGridSpec(
            num_scalar_prefetch=3,
            grid=(n_tiles, nj),
            in_specs=[
                pl.BlockSpec((tm, d), lambda i, j, te, nu, live: (tile(i, nu), 0)),
                pl.BlockSpec((None, d, tc), lambda i, j, te, nu, live: (te[i], 0, chunk(i, j, nu))),
                pl.BlockSpec((None, 1, tc), lambda i, j, te, nu, live: (te[i], 0, chunk(i, j, nu))),
                pl.BlockSpec((None, d, tc), lambda i, j, te, nu, live: (te[i], 0, chunk(i, j, nu))),
                pl.BlockSpec((None, 1, tc), lambda i, j, te, nu, live: (te[i], 0, chunk(i, j, nu))),
                pl.BlockSpec((None, tc, d), lambda i, j, te, nu, live: (te[i], chunk(i, j, nu), 0)),
                pl.BlockSpec((None, 1, d), lambda i, j, te, nu, live: (te[i], 0, 0)),
            ],
            out_specs=pl.BlockSpec((tm, d), lambda i, j, te, nu, live: (i, 0)),
            scratch_shapes=[pltpu.VMEM((tm, d), BF16)],
        ),
        compiler_params=_cparams(("arbitrary", "arbitrary"), BIG_VMEM_LIMIT),
        name="experts",
    )(tile_expert, n_used, tile_live, xs, wg, bg, wu, bu, wd, bd)


def _combine_body(dcur_ref, dnext_ref, gate_ref, h_ref, gfin_ref, ys_ref, o_ref, buf, sems, *, final_norm):
    i = pl.program_id(0)
    nb = pl.num_programs(0)
    tm = h_ref.shape[0]
    slot = i % 2

    def start_all(dref, s):
        def go(g, carry):
            t0 = pl.multiple_of(g * SUBLANES, SUBLANES)
            for r in range(SUBLANES):
                for k in range(TOP_K):
                    pltpu.make_async_copy(ys_ref.at[pl.ds(dref[0, (t0 + r) * TOP_K + k], 1), :],
                                          buf.at[s, k, pl.ds(t0 + r, 1), :], sems.at[s]).start()
            return carry
        lax.fori_loop(0, tm // SUBLANES, go, 0)

    @pl.when(i == 0)
    def _():
        start_all(dcur_ref, 0)

    @pl.when(i + 1 < nb)
    def _():
        start_all(dnext_ref, 1 - slot)

    for k in range(TOP_K):
        pltpu.make_async_copy(ys_ref.at[pl.ds(0, tm), :], buf.at[slot, k], sems.at[slot]).wait()

    y = h_ref[...]
    gates = gate_ref[...]
    for k in range(TOP_K):
        y = y + gates[:, k:k + 1] * buf[slot, k]
    if final_norm:
        ms = jnp.mean(y * y, axis=-1, keepdims=True)
        y = y * lax.rsqrt(ms + NORM_EPS) * gfin_ref[...]
    o_ref[...] = y


def _combine(dest, gates, h, g_final, ys, *, final_norm):
    n, d = h.shape
    tm = COMBINE_TM
    nb = n // tm
    dest3 = dest.reshape(nb, 1, tm * TOP_K)
    smem_blk = (None, 1, tm * TOP_K)
    return pl.pallas_call(
        functools.partial(_combine_body, final_norm=final_norm),
        out_shape=jax.ShapeDtypeStruct((n, d), F32),
        grid=(nb,),
        in_specs=[
            pl.BlockSpec(smem_blk, lambda i: (i, 0, 0), memory_space=pltpu.SMEM),
            pl.BlockSpec(smem_blk, lambda i: (jnp.minimum(i + 1, nb - 1), 0, 0), memory_space=pltpu.SMEM),
            pl.BlockSpec((tm, LANES), lambda i: (i, 0)),
            pl.BlockSpec((tm, d), lambda i: (i, 0)),
            pl.BlockSpec((1, d), lambda i: (0, 0)),
            pl.BlockSpec(memory_space=pl.ANY),
        ],
        out_specs=pl.BlockSpec((tm, d), lambda i: (i, 0)),
        scratch_shapes=[pltpu.VMEM((2, TOP_K, tm, d), F32), pltpu.SemaphoreType.DMA((2,))],
        compiler_params=_cparams(("arbitrary",)),
        name="combine",
    )(dest3, dest3, gates, h, g_final, ys)


def _rope_tables(positions):
    inv_freq = ROPE_THETA ** (-jnp.arange(0, HEAD_DIM, 2, dtype=F32) / HEAD_DIM)
    ang = positions.astype(F32)[..., None] * inv_freq
    cos, sin = jnp.cos(ang), jnp.sin(ang)
    return jnp.concatenate([cos, cos], axis=-1), jnp.concatenate([-sin, sin], axis=-1)


def _layer(x2, cos, sin_signed, layer_idx, bsz, seq, g_attn_norm, w_in, b_forget, lambda_q1, lambda_k1,
           lambda_q2, lambda_k2, g_diff_sub, g_fox_out, w_out, g_ffn_norm, w_router, b_router,
           w_gate, b_gate, w_up, b_up, w_down, b_down):
    n, d = x2.shape
    diff_w = d // 2
    fox_w = d - diff_w
    fox_heads = fox_w // HEAD_DIM
    main_cols = 3 * diff_w + 3 * fox_w
    n_experts = w_router.shape[1]
    lam_init = _lambda_init(layer_idx)

    d_exp = w_gate.shape[2]
    hosted = {"diff": (w_down.reshape(n_experts * d_exp, d), w_up.reshape(n_experts * d, d_exp)),
              "fox": (w_gate.reshape(n_experts * d, d_exp),)}
    host_steps = {"diff": bsz * (diff_w // (2 * HEAD_DIM)) * (seq // min(DIFF_T, seq)),
                  "fox": bsz * fox_heads * (seq // min(FOX_T, seq))}
    fused = {k: _can_cast_rows(ws, host_steps[k]) for k, ws in hosted.items()}

    w_in_t = jnp.swapaxes(w_in, 0, 1)
    w_main_t = w_in_t[:main_cols].astype(BF16)
    w_f_hi, w_f_lo = _split_bf16(jnp.pad(w_in_t[main_cols:], ((0, LANES - fox_heads), (0, 0))))
    proj, f_logit = _inproj(x2, g_attn_norm[None, :], w_main_t, w_f_hi, w_f_lo, cos, sin_signed, (),
                            diff_w=diff_w, fox_w=fox_w)

    z_rows = f_logit[:, :fox_heads].reshape(bsz, seq, fox_heads).transpose(0, 2, 1).reshape(bsz * fox_heads, seq)
    b_rows = jnp.tile(b_forget.astype(F32), bsz)[:, None]
    c_rows = _cumgate(z_rows, b_rows)

    od, *diff_cast = _diff_attention(proj, lambda_q1[None, :], lambda_k1[None, :], lambda_q2[None, :],
                                     lambda_k2[None, :], g_diff_sub[None, :], hosted["diff"] if fused["diff"] else (),
                                     bsz=bsz, seq=seq, diff_w=diff_w, lam_init=lam_init)
    c_keys = (c_rows * LOG2E).reshape(bsz * fox_heads, seq // LANES, LANES)
    of, *fox_cast = _fox_attention(proj, c_keys, g_fox_out[None, :], hosted["fox"] if fused["fox"] else (),
                                   bsz=bsz, seq=seq, diff_w=diff_w, fox_w=fox_w)
    w_down_b, w_up_b = diff_cast if fused["diff"] else (w.astype(BF16) for w in hosted["diff"])
    (w_gate_b,) = fox_cast if fused["fox"] else (w.astype(BF16) for w in hosted["fox"])
    w_down_b, w_up_b, w_gate_b = (w_down_b.reshape(w_down.shape), w_up_b.reshape(w_up.shape),
                                  w_gate_b.reshape(w_gate.shape))

    w_out_b = w_out.astype(BF16)
    w_router_hi, w_router_lo = _split_bf16(jnp.pad(w_router, ((0, 0), (0, LANES - n_experts))))
    b_router_p = jnp.pad(b_router, (0, LANES - n_experts))[None, :]
    h, hn, logits = _outproj(od, of, w_out_b[:diff_w], w_out_b[diff_w:], x2, g_ffn_norm[None, :],
                             w_router_hi, w_router_lo, b_router_p)

    e_pad, gate_pad, rank_pad, cnt = _route(logits, n_experts)
    e_idx = e_pad[:, :TOP_K]
    counts = cnt[0, :n_experts].astype(jnp.int32)
    padded = (counts + EXPERT_TM - 1) // EXPERT_TM * EXPERT_TM
    pad_ends = jnp.cumsum(padded)
    pad_starts = pad_ends - padded
    start_of = jnp.sum(jnp.where(e_idx[..., None] == jnp.arange(n_experts), pad_starts, 0), axis=-1)
    dest = (start_of + rank_pad[:, :TOP_K]).astype(jnp.int32)
    n_tiles = -(-n * TOP_K // EXPERT_TM) + n_experts
    tile_start = jnp.arange(n_tiles, dtype=jnp.int32) * EXPERT_TM
    tile_expert = jnp.sum((pad_ends[None, :] <= tile_start[:, None]).astype(jnp.int32), axis=1)
    tile_expert = jnp.minimum(tile_expert, n_experts - 1)
    n_used = (pad_ends[-1:] // EXPERT_TM).astype(jnp.int32)
    tile_expert = jnp.where(jnp.arange(n_tiles) < n_used[0], tile_expert, tile_expert[n_used[0] - 1])
    last_tile = jnp.where(padded > 0, pad_ends // EXPERT_TM - 1, -1)
    tail_tile = n_used[0] + jnp.arange(n_tiles - n * TOP_K // EXPERT_TM, dtype=jnp.int32)
    tail_tile = jnp.where(tail_tile < n_tiles, tail_tile, -1)
    zero_tiles = jnp.concatenate([last_tile, tail_tile]).astype(jnp.int32)

    routed_end = (pad_starts + counts)[tile_expert]
    tile_live = jnp.clip(routed_end - tile_start, 0, EXPERT_TM)
    tile_live = jnp.where(jnp.arange(n_tiles) < n_used[0], tile_live, 0)
    tile_live = ((tile_live + EXPERT_RT - 1) // EXPERT_RT * EXPERT_RT).astype(jnp.int32)

    xs = _dispatch(zero_tiles, dest, hn, n_tiles * EXPERT_TM)
    ys = _experts(tile_expert, n_used, tile_live, xs, w_gate_b, b_gate[:, None, :], w_up_b, b_up[:, None, :],
                  w_down_b, b_down[:, None, :])
    return dest, gate_pad, h, ys


def kernel(x, positions, g_attn_norm, w_in, b_forget, lambda_q1, lambda_k1, lambda_q2, lambda_k2, g_diff_sub,
           g_fox_out, w_out, g_ffn_norm, w_router, b_router, w_gate, b_gate, w_up, b_up, w_down, b_down,
           g_final):
    bsz, seq, d = x.shape
    depth = w_in.shape[0]
    cos, sin_signed = _rope_tables(positions)
    cos = cos.reshape(bsz * seq, HEAD_DIM)
    sin_signed = sin_signed.reshape(bsz * seq, HEAD_DIM)
    x2 = x.reshape(bsz * seq, d)
    for l in range(depth):
        dest, gates, h, ys = _layer(
            x2, cos, sin_signed, l, bsz, seq, g_attn_norm[l], w_in[l], b_forget[l], lambda_q1[l], lambda_k1[l],
            lambda_q2[l], lambda_k2[l], g_diff_sub[l], g_fox_out[l], w_out[l], g_ffn_norm[l], w_router[l],
            b_router[l], w_gate[l], b_gate[l], w_up[l], b_up[l], w_down[l], b_down[l])
        x2 = _combine(dest, gates, h, g_final[None, :], ys, final_norm=(l == depth - 1))
    return x2.reshape(bsz, seq, d)
```

```python
import functools
import math

import jax
import jax.numpy as jnp
from jax import lax
from jax.experimental import pallas as pl
from jax.experimental.pallas import tpu as pltpu

F32 = jnp.float32
BF16 = jnp.bfloat16

HEAD_DIM = 128
CHUNK = 64
ROPE_THETA = 10000.0
NORM_EPS = 1e-5
NEG_BIG = -1e30
TOP_K = 4
SWIGLU_LIMIT = 7.0
SWIGLU_ALPHA = 1.702
LANES = 128
SUBLANES = 8
QK_SCALE = HEAD_DIM ** -0.5
LOG2E = math.log2(math.e)
ONES_ROWS = 16

INPROJ_TM = 1024
INPROJ_TN = 512
DIFF_T = 512
FOX_T = 1024
CAST_ROWS = 512
VT_CHUNK = 512
OUTPROJ_TM = 512
ROUTE_TM = 512
DISPATCH_TM = 256
EXPERT_TM = 512
EXPERT_RT = 128
EXPERT_TC = 1024
COMBINE_TM = 128
VMEM_LIMIT = 48 * 1024 * 1024
BIG_VMEM_LIMIT = 58 * 1024 * 1024


def _lambda_init(layer_idx):
    return 0.8 - 0.6 * math.exp(-0.3 * layer_idx)


def _cparams(sem, vmem_limit=VMEM_LIMIT):
    return pltpu.CompilerParams(dimension_semantics=sem, vmem_limit_bytes=vmem_limit)


def _split_bf16(x):
    hi = x.astype(BF16)
    return hi, (x - hi.astype(F32)).astype(BF16)


def _dot_nt(a, b):
    return lax.dot_general(a, b, (((1,), (1,)), ((), ())), preferred_element_type=F32)


def _dot3(x_hi, x_lo, w_hi, w_lo, dot=functools.partial(jnp.dot, preferred_element_type=F32)):
    return dot(x_hi, w_hi) + (dot(x_hi, w_lo) + dot(x_lo, w_hi))


def _cast_rows_specs(weights, step_of):
    in_specs, out_specs, out_shapes = [], [], []
    for w in weights:
        rows, cols = w.shape
        last = rows // CAST_ROWS - 1
        spec = pl.BlockSpec((CAST_ROWS, cols), lambda *idx, last=last: (jnp.minimum(step_of(*idx), last), 0))
        in_specs.append(spec)
        out_specs.append(spec)
        out_shapes.append(jax.ShapeDtypeStruct((rows, cols), BF16))
    return in_specs, out_specs, out_shapes


def _can_cast_rows(weights, steps):
    return all(w.shape[0] % CAST_ROWS == 0 and w.shape[0] // CAST_ROWS <= steps for w in weights)


def _linear_step(n_axes):
    step = pl.program_id(0)
    for axis in range(1, n_axes):
        step = step * pl.num_programs(axis) + pl.program_id(axis)
    return step


def _cast_rows(step, in_refs, out_refs, total_rows):
    for src, dst, rows in zip(in_refs, out_refs, total_rows):
        @pl.when(step < rows // CAST_ROWS)
        def _():
            dst[...] = src[...].astype(BF16)


def _inproj_body(x_ref, g_ref, w_ref, wfh_ref, wfl_ref, cos_ref, sin_ref, o_ref, fl_ref, xn_ref,
                 *, n_rope, tn, q_ranges):
    j = pl.program_id(1)

    @pl.when(j == 0)
    def _():
        x = x_ref[...]
        ms = jnp.mean(x * x, axis=-1, keepdims=True)
        xn = x * lax.rsqrt(ms + NORM_EPS) * g_ref[...]
        xn_hi, xn_lo = _split_bf16(xn)
        xn_ref[...] = xn_hi
        fl_ref[...] = _dot3(xn_hi, xn_lo, wfh_ref[...], wfl_ref[...], dot=_dot_nt)

    acc = _dot_nt(xn_ref[...], w_ref[...])
    col0 = j * tn
    is_q = (col0 >= q_ranges[0][0]) & (col0 < q_ranges[0][1])
    is_q = is_q | ((col0 >= q_ranges[1][0]) & (col0 < q_ranges[1][1]))
    scale = jnp.where(is_q, QK_SCALE * LOG2E, 1.0).astype(F32)

    @pl.when(j < n_rope)
    def _():
        cos = cos_ref[...] * scale
        sin = sin_ref[...] * scale
        for gi in range(tn // LANES):
            xg = acc[:, gi * LANES:(gi + 1) * LANES]
            rot = pltpu.roll(xg, HEAD_DIM // 2, 1)
            o_ref[:, gi * LANES:(gi + 1) * LANES] = (xg * cos + rot * sin).astype(BF16)

    @pl.when(j >= n_rope)
    def _():
        o_ref[...] = (acc * scale).astype(BF16)


def _inproj(x2, g, w_main_t, w_f_hi, w_f_lo, cos, sin_signed, *, diff_w, fox_w):
    n, d = x2.shape
    cols = w_main_t.shape[0]
    tm, tn = min(INPROJ_TM, n), INPROJ_TN
    n_rope = 2 * diff_w // tn
    q_ranges = ((0, diff_w), (3 * diff_w, 3 * diff_w + fox_w))
    body = functools.partial(_inproj_body, n_rope=n_rope, tn=tn, q_ranges=q_ranges)
    return pl.pallas_call(
        body,
        out_shape=(jax.ShapeDtypeStruct((n, cols), BF16), jax.ShapeDtypeStruct((n, LANES), F32)),
        grid=(n // tm, cols // tn),
        in_specs=[
            pl.BlockSpec((tm, d), lambda i, j: (i, 0)),
            pl.BlockSpec((1, d), lambda i, j: (0, 0)),
            pl.BlockSpec((tn, d), lambda i, j: (j, 0)),
            pl.BlockSpec((LANES, d), lambda i, j: (0, 0)),
            pl.BlockSpec((LANES, d), lambda i, j: (0, 0)),
            pl.BlockSpec((tm, LANES), lambda i, j: (i, 0)),
            pl.BlockSpec((tm, LANES), lambda i, j: (i, 0)),
        ],
        out_specs=(pl.BlockSpec((tm, tn), lambda i, j: (i, j)),
                   pl.BlockSpec((tm, LANES), lambda i, j: (i, 0))),
        scratch_shapes=[pltpu.VMEM((tm, d), BF16)],
        compiler_params=_cparams(("parallel", "arbitrary")),
        name="inproj",
    )(x2, g, w_main_t, w_f_hi, w_f_lo, cos, sin_signed)


def _cumgate_body(z_ref, b_ref, c_ref):
    z = z_ref[...] + b_ref[...]
    v = jnp.minimum(z, 0.0) - jnp.log(1.0 + jnp.exp(-jnp.abs(z)))
    seq = v.shape[1]
    lane = lax.broadcasted_iota(jnp.int32, v.shape, 1)
    shift = 1
    while shift < seq:
        v = v + jnp.where(lane >= shift, pltpu.roll(v, shift, 1), 0.0)
        shift *= 2
    c_ref[...] = v


def _cumgate(z_rows, b_rows):
    return pl.pallas_call(
        _cumgate_body,
        out_shape=jax.ShapeDtypeStruct(z_rows.shape, F32),
        name="cumgate",
    )(z_rows, b_rows)


def _build_vt(v_ref, vt_ref):
    seq, dv = v_ref.shape
    for c in range(seq // VT_CHUNK):
        blk = v_ref[c * VT_CHUNK:(c + 1) * VT_CHUNK, :].astype(F32)
        vt_ref[0:dv, c * VT_CHUNK:(c + 1) * VT_CHUNK] = blk.T.astype(BF16)
    vt_ref[dv:dv + ONES_ROWS, :] = jnp.ones((ONES_ROWS, seq), BF16)


def _softmax_step(s_t, vt_blk, m_ref, acc_ref):
    m_prev = m_ref[...]
    m_new = jnp.maximum(m_prev, jnp.max(s_t, axis=0, keepdims=True))
    alpha = jnp.exp2(m_prev - m_new)
    p_t = jnp.exp2(s_t - m_new).astype(BF16)
    acc_ref[...] = alpha * acc_ref[...] + jnp.dot(vt_blk, p_t, preferred_element_type=F32)
    m_ref[...] = m_new


def _sweep_keys(qi, t, step):
    def pair(i, carry):
        step(pl.multiple_of(i * (2 * t), 2 * t), 2 * t, False)
        return carry

    lax.fori_loop(0, qi // 2, pair, 0)

    @pl.when(qi % 2 == 1)
    def _():
        step(pl.multiple_of((qi - 1) * t, t), t, False)

    step(pl.multiple_of(qi * t, t), t, True)


def _normalised(acc, dv):
    return acc[0:dv] * (1.0 / acc[dv:dv + 1])


def _diff_body(lq1_ref, lk1_ref, lq2_ref, lk2_ref, gsub_ref, q1_ref, q2_ref, k1_ref, k2_ref, v_ref,
               *rest, t, lam_init, cast_rows):
    n_cast = len(cast_rows)
    cast_in, (o_ref, *cast_out), (vt, m1, a1, m2, a2) = rest[:n_cast], rest[n_cast:2 * n_cast + 1], rest[2 * n_cast + 1:]
    _cast_rows(_linear_step(3), cast_in, cast_out, cast_rows)
    qi = pl.program_id(2)
    dv = v_ref.shape[1]

    @pl.when(qi == 0)
    def _():
        _build_vt(v_ref, vt)

    for m_ref, a_ref in ((m1, a1), (m2, a2)):
        m_ref[...] = jnp.full(m_ref.shape, NEG_BIG, F32)
        a_ref[...] = jnp.zeros(a_ref.shape, F32)
    q1 = q1_ref[...]
    q2 = q2_ref[...]
    key_chunk = lax.broadcasted_iota(jnp.int32, (t, t), 0) // CHUNK
    qry_chunk = lax.broadcasted_iota(jnp.int32, (t, t), 1) // CHUNK
    visible = key_chunk <= qry_chunk

    def step(off, size, diagonal):
        vt_blk = vt[:, pl.ds(off, size)]
        for q, k_ref, m_ref, a_ref in ((q1, k1_ref, m1, a1), (q2, k2_ref, m2, a2)):
            s_t = _dot_nt(k_ref[pl.ds(off, size), :], q)
            if diagonal:
                s_t = jnp.where(visible, s_t, NEG_BIG)
            _softmax_step(s_t, vt_blk, m_ref, a_ref)

    _sweep_keys(qi, t, step)

    lam = (jnp.exp(jnp.sum(lq1_ref[...] * lk1_ref[...], axis=-1, keepdims=True))
           - jnp.exp(jnp.sum(lq2_ref[...] * lk2_ref[...], axis=-1, keepdims=True)) + lam_init)
    o = (_normalised(a1[...], dv) - lam * _normalised(a2[...], dv)).T
    ms = jnp.mean(o * o, axis=-1, keepdims=True)
    y = o * lax.rsqrt(ms + NORM_EPS) * gsub_ref[...]
    o_ref[...] = (y * (1.0 - lam_init)).astype(BF16)


def _diff_attention(proj, lq1, lk1, lq2, lk2, gsub, cast_weights, *, bsz, seq, diff_w, lam_init):
    t = min(DIFF_T, seq)
    dv = 2 * HEAD_DIM
    n_heads = diff_w // dv
    nq = seq // t
    kcol = diff_w // HEAD_DIM
    vcol = 2 * diff_w // dv
    vec = pl.BlockSpec((1, HEAD_DIM), lambda b, h, i: (0, 0))
    cast_in, cast_out, cast_shapes = _cast_rows_specs(cast_weights, lambda b, h, i: (b * n_heads + h) * nq + i)
    body = functools.partial(_diff_body, t=t, lam_init=lam_init, cast_rows=tuple(w.shape[0] for w in cast_weights))
    return pl.pallas_call(
        body,
        out_shape=(jax.ShapeDtypeStruct((bsz * seq, diff_w), BF16), *cast_shapes),
        grid=(bsz, n_heads, nq),
        in_specs=[
            vec, vec, vec, vec,
            pl.BlockSpec((1, dv), lambda b, h, i: (0, 0)),
            pl.BlockSpec((t, HEAD_DIM), lambda b, h, i: (b * nq + i, 2 * h)),
            pl.BlockSpec((t, HEAD_DIM), lambda b, h, i: (b * nq + i, 2 * h + 1)),
            pl.BlockSpec((seq, HEAD_DIM), lambda b, h, i: (b, kcol + 2 * h)),
            pl.BlockSpec((seq, HEAD_DIM), lambda b, h, i: (b, kcol + 2 * h + 1)),
            pl.BlockSpec((seq, dv), lambda b, h, i: (b, vcol + h)),
            *cast_in,
        ],
        out_specs=(pl.BlockSpec((t, dv), lambda b, h, i: (b * nq + i, h)), *cast_out),
        scratch_shapes=[pltpu.VMEM((dv + ONES_ROWS, seq), BF16),
                        pltpu.VMEM((1, t), F32), pltpu.VMEM((dv + ONES_ROWS, t), F32),
                        pltpu.VMEM((1, t), F32), pltpu.VMEM((dv + ONES_ROWS, t), F32)],
        compiler_params=_cparams(("arbitrary", "arbitrary", "arbitrary"), BIG_VMEM_LIMIT),
        name="diffattn",
    )(lq1, lk1, lq2, lk2, gsub, proj, proj, proj, proj, proj, *cast_weights)


def _fox_body(g_ref, q_ref, k_ref, ck_ref, v_ref, *rest, t, cast_rows):
    n_cast = len(cast_rows)
    cast_in, (o_ref, *cast_out), (vt, ck_lanes, m, acc) = rest[:n_cast], rest[n_cast:2 * n_cast + 1], rest[2 * n_cast + 1:]
    _cast_rows(_linear_step(3), cast_in, cast_out, cast_rows)
    qi = pl.program_id(2)
    dv = v_ref.shape[1]

    @pl.when(qi == 0)
    def _():
        _build_vt(v_ref, vt)
        for j in range(ck_ref.shape[0]):
            ck_lanes[j * LANES:(j + 1) * LANES, :] = jnp.broadcast_to(ck_ref[j:j + 1, :], (LANES, LANES)).T

    m[...] = jnp.full(m.shape, NEG_BIG, F32)
    acc[...] = jnp.zeros(acc.shape, F32)
    q = q_ref[...]
    causal = lax.broadcasted_iota(jnp.int32, (t, t), 0) <= lax.broadcasted_iota(jnp.int32, (t, t), 1)

    def step(off, size, diagonal):
        ck = ck_lanes[pl.ds(off, size), :]
        s_t = _dot_nt(k_ref[pl.ds(off, size), :], q) - jnp.concatenate([ck] * (t // LANES), axis=1)
        if diagonal:
            s_t = jnp.where(causal, s_t, NEG_BIG)
        _softmax_step(s_t, vt[:, pl.ds(off, size)], m, acc)

    _sweep_keys(qi, t, step)

    o = _normalised(acc[...], dv).T
    ms = jnp.mean(o * o, axis=-1, keepdims=True)
    o_ref[...] = (o * lax.rsqrt(ms + NORM_EPS) * g_ref[...]).astype(BF16)


def _fox_attention(proj, c_keys, g_fox, cast_weights, *, bsz, seq, diff_w, fox_w):
    t = min(FOX_T, seq)
    n_heads = fox_w // HEAD_DIM
    nq = seq // t
    qcol = 3 * diff_w // HEAD_DIM
    kcol = qcol + n_heads
    vcol = kcol + n_heads
    cast_in, cast_out, cast_shapes = _cast_rows_specs(cast_weights, lambda b, h, i: (b * n_heads + h) * nq + i)
    body = functools.partial(_fox_body, t=t, cast_rows=tuple(w.shape[0] for w in cast_weights))
    return pl.pallas_call(
        body,
        out_shape=(jax.ShapeDtypeStruct((bsz * seq, fox_w), BF16), *cast_shapes),
        grid=(bsz, n_heads, nq),
        in_specs=[
            pl.BlockSpec((1, HEAD_DIM), lambda b, h, i: (0, 0)),
            pl.BlockSpec((t, HEAD_DIM), lambda b, h, i: (b * nq + i, qcol + h)),
            pl.BlockSpec((seq, HEAD_DIM), lambda b, h, i: (b, kcol + h)),
            pl.BlockSpec((None, seq // LANES, LANES), lambda b, h, i: (b * n_heads + h, 0, 0)),
            pl.BlockSpec((seq, HEAD_DIM), lambda b, h, i: (b, vcol + h)),
            *cast_in,
        ],
        out_specs=(pl.BlockSpec((t, HEAD_DIM), lambda b, h, i: (b * nq + i, h)), *cast_out),
        scratch_shapes=[pltpu.VMEM((HEAD_DIM + ONES_ROWS, seq), BF16), pltpu.VMEM((seq, LANES), F32),
                        pltpu.VMEM((1, t), F32), pltpu.VMEM((HEAD_DIM + ONES_ROWS, t), F32)],
        compiler_params=_cparams(("arbitrary", "arbitrary", "arbitrary"), BIG_VMEM_LIMIT),
        name="foxattn",
    )(g_fox, proj, proj, c_keys, proj, *cast_weights)


def _outproj_body(od_ref, of_ref, wd_ref, wf_ref, x_ref, g_ref, wrh_ref, wrl_ref, br_ref, h_ref, hn_ref, lg_ref):
    h = (x_ref[...]
         + jnp.dot(od_ref[...], wd_ref[...], preferred_element_type=F32)
         + jnp.dot(of_ref[...], wf_ref[...], preferred_element_type=F32))
    h_ref[...] = h
    ms = jnp.mean(h * h, axis=-1, keepdims=True)
    hn = h * lax.rsqrt(ms + NORM_EPS) * g_ref[...]
    hn_ref[...] = hn
    hn_hi, hn_lo = _split_bf16(hn)
    lg_ref[...] = _dot3(hn_hi, hn_lo, wrh_ref[...], wrl_ref[...]) + br_ref[...]


def _outproj(od, of, w_od, w_of, x2, g_ffn, w_router_hi, w_router_lo, b_router_p):
    n, d = x2.shape
    tm = OUTPROJ_TM
    row = lambda i: (i, 0)
    const = lambda i: (0, 0)
    resident = functools.partial(pl.BlockSpec, index_map=const, pipeline_mode=pl.Buffered(1))
    return pl.pallas_call(
        _outproj_body,
        out_shape=(jax.ShapeDtypeStruct((n, d), F32), jax.ShapeDtypeStruct((n, d), F32),
                   jax.ShapeDtypeStruct((n, LANES), F32)),
        grid=(n // tm,),
        in_specs=[
            pl.BlockSpec((tm, od.shape[1]), row),
            pl.BlockSpec((tm, of.shape[1]), row),
            resident(w_od.shape),
            resident(w_of.shape),
            pl.BlockSpec((tm, d), row),
            pl.BlockSpec((1, d), const),
            resident((d, LANES)),
            resident((d, LANES)),
            pl.BlockSpec((1, LANES), const),
        ],
        out_specs=(pl.BlockSpec((tm, d), row), pl.BlockSpec((tm, d), row), pl.BlockSpec((tm, LANES), row)),
        compiler_params=_cparams(("parallel",)),
        name="outproj",
    )(od, of, w_od, w_of, x2, g_ffn, w_router_hi, w_router_lo, b_router_p)


def _route_body(lg_ref, e_ref, gate_ref, rank_ref, cnt_ref, carry_ref, *, n_experts):
    i = pl.program_id(0)
    tm = lg_ref.shape[0]

    @pl.when(i == 0)
    def _():
        carry_ref[...] = jnp.zeros(carry_ref.shape, F32)

    lane = lax.broadcasted_iota(jnp.int32, (tm, LANES), 1)
    logits = jnp.where(lane < n_experts, lg_ref[...], -jnp.inf)
    vals, idxs, hots = [], [], []
    for _ in range(TOP_K):
        mx = jnp.max(logits, axis=-1, keepdims=True)
        idx = jnp.min(jnp.where(logits == mx, lane, LANES), axis=-1, keepdims=True)
        hot = lane == idx
        logits = jnp.where(hot, -jnp.inf, logits)
        vals.append(mx)
        idxs.append(idx)
        hots.append(hot)

    exps = [jnp.exp(v - vals[0]) for v in vals]
    denom = exps[0]
    for ex in exps[1:]:
        denom = denom + ex
    inv = 1.0 / denom

    hot_sum = jnp.zeros((tm, LANES), F32)
    for hot in hots:
        hot_sum = hot_sum + jnp.where(hot, 1.0, 0.0)
    r = lax.broadcasted_iota(jnp.int32, (tm, tm), 0)
    c = lax.broadcasted_iota(jnp.int32, (tm, tm), 1)
    earlier = jnp.where(c < r, 1.0, 0.0).astype(BF16)
    base = carry_ref[...] + jnp.dot(earlier, hot_sum.astype(BF16), preferred_element_type=F32)

    e_out = jnp.zeros((tm, LANES), jnp.int32)
    g_out = jnp.zeros((tm, LANES), F32)
    r_out = jnp.zeros((tm, LANES), jnp.int32)
    for k in range(TOP_K):
        rank = jnp.sum(jnp.where(hots[k], base, 0.0), axis=-1, keepdims=True).astype(jnp.int32)
        e_out = jnp.where(lane == k, idxs[k], e_out)
        g_out = jnp.where(lane == k, exps[k] * inv, g_out)
        r_out = jnp.where(lane == k, rank, r_out)
    e_ref[...] = e_out
    gate_ref[...] = g_out
    rank_ref[...] = r_out
    carry_ref[...] = carry_ref[...] + jnp.sum(hot_sum, axis=0, keepdims=True)
    cnt_ref[...] = carry_ref[...]


def _route(logits, n_experts):
    n = logits.shape[0]
    tm = ROUTE_TM
    row = lambda i: (i, 0)
    body = functools.partial(_route_body, n_experts=n_experts)
    return pl.pallas_call(
        body,
        out_shape=(jax.ShapeDtypeStruct((n, LANES), jnp.int32), jax.ShapeDtypeStruct((n, LANES), F32),
                   jax.ShapeDtypeStruct((n, LANES), jnp.int32), jax.ShapeDtypeStruct((1, LANES), F32)),
        grid=(n // tm,),
        in_specs=[pl.BlockSpec((tm, LANES), row)],
        out_specs=(pl.BlockSpec((tm, LANES), row), pl.BlockSpec((tm, LANES), row),
                   pl.BlockSpec((tm, LANES), row), pl.BlockSpec((1, LANES), lambda i: (0, 0))),
        scratch_shapes=[pltpu.VMEM((1, LANES), F32)],
        compiler_params=_cparams(("arbitrary",)),
        name="route",
    )(logits)


def _dispatch_body(ztile_ref, dest_ref, hn_ref, xs_ref, zeros, sem, zsem):
    tm = hn_ref.shape[0]

    @pl.when(pl.program_id(0) == 0)
    def _():
        zeros[...] = jnp.zeros(zeros.shape, zeros.dtype)

        def fill(z):
            row0 = pl.multiple_of(ztile_ref[z] * EXPERT_TM, EXPERT_TM)
            return pltpu.make_async_copy(zeros, xs_ref.at[pl.ds(row0, EXPERT_TM), :], zsem)

        def fill_start(z, carry):
            @pl.when(ztile_ref[z] >= 0)
            def _():
                fill(z).start()
            return carry

        def fill_wait(z, carry):
            @pl.when(ztile_ref[z] >= 0)
            def _():
                fill(z).wait()
            return carry

        lax.fori_loop(0, ztile_ref.shape[0], fill_start, 0)
        lax.fori_loop(0, ztile_ref.shape[0], fill_wait, 0)

    def start(t, carry):
        src = hn_ref.at[pl.ds(t, 1), :]
        for k in range(TOP_K):
            pltpu.make_async_copy(src, xs_ref.at[pl.ds(dest_ref[0, t * TOP_K + k], 1), :], sem).start()
        return carry

    lax.fori_loop(0, tm, start, 0)
    for _ in range(TOP_K):
        pltpu.make_async_copy(hn_ref, xs_ref.at[pl.ds(0, tm), :], sem).wait()


def _dispatch(zero_tiles, dest, hn, n_rows):
    n, d = hn.shape
    tm = DISPATCH_TM
    dest3 = dest.reshape(n // tm, 1, tm * TOP_K)
    return pl.pallas_call(
        _dispatch_body,
        out_shape=jax.ShapeDtypeStruct((n_rows, d), hn.dtype),
        grid_spec=pltpu.PrefetchScalarGridSpec(
            num_scalar_prefetch=1,
            grid=(n // tm,),
            in_specs=[
                pl.BlockSpec((None, 1, tm * TOP_K), lambda i, zt: (i, 0, 0), memory_space=pltpu.SMEM),
                pl.BlockSpec((tm, d), lambda i, zt: (i, 0)),
            ],
            out_specs=pl.BlockSpec(memory_space=pl.ANY),
            scratch_shapes=[pltpu.VMEM((EXPERT_TM, d), hn.dtype), pltpu.SemaphoreType.DMA,
                            pltpu.SemaphoreType.DMA],
        ),
        compiler_params=_cparams(("arbitrary",)),
        name="dispatch",
    )(zero_tiles, dest3, hn)


def _experts_body(te_ref, nu_ref, live_ref, x_ref, wg_ref, bg_ref, wu_ref, bu_ref, wd_ref, bd_ref, o_ref, xb_ref):
    i = pl.program_id(0)
    j = pl.program_id(1)
    tm = x_ref.shape[0]

    def compute(r):
        @pl.when(j == 0)
        def _():
            xb_ref[0:r, :] = x_ref[0:r, :].astype(BF16)
            o_ref[0:r, :] = jnp.broadcast_to(bd_ref[...], (r, o_ref.shape[1]))
            if r < tm:
                o_ref[r:tm, :] = jnp.zeros((tm - r, o_ref.shape[1]), F32)

        xb = xb_ref[0:r, :]
        g = jnp.dot(xb, wg_ref[...], preferred_element_type=F32) + bg_ref[...]
        u = jnp.dot(xb, wu_ref[...], preferred_element_type=F32) + bu_ref[...]
        g = jnp.minimum(g, SWIGLU_LIMIT)
        u = jnp.clip(u, -SWIGLU_LIMIT, SWIGLU_LIMIT)
        hdn = g * (1.0 / (1.0 + jnp.exp(-SWIGLU_ALPHA * g))) * (u + 1.0)
        o_ref[0:r, :] += jnp.dot(hdn.astype(BF16), wd_ref[...], preferred_element_type=F32)

    for r in range(EXPERT_RT, tm + 1, EXPERT_RT):
        pl.when(live_ref[i] == r)(functools.partial(compute, r))

    @pl.when((live_ref[i] == 0) & (j == 0))
    def _():
        o_ref[...] = jnp.zeros(o_ref.shape, F32)


def _experts(tile_expert, n_used, tile_live, xs, wg, bg, wu, bu, wd, bd):
    rows, d = xs.shape
    d_exp = wg.shape[2]
    tm, tc = EXPERT_TM, min(EXPERT_TC, d_exp)
    n_tiles = rows // tm
    nj = d_exp // tc

    def tile(i, nu):
        return jnp.minimum(i, nu[0] - 1)

    def chunk(i, j, nu):
        return jnp.where(i < nu[0], j, nj - 1)

    return pl.pallas_call(
        _experts_body,
        out_shape=jax.ShapeDtypeStruct((rows, d), F32),
        grid_spec=pltpu.PrefetchScalarGridSpec(
            num_scalar_prefetch=3,
            grid=(n_tiles, nj),
            in_specs=[
                pl.BlockSpec((tm, d), lambda i, j, te, nu, live: (tile(i, nu), 0)),
                pl.BlockSpec((None, d, tc), lambda i, j, te, nu, live: (te[i], 0, chunk(i, j, nu))),
                pl.BlockSpec((None, 1, tc), lambda i, j, te, nu, live: (te[i], 0, chunk(i, j, nu))),
                pl.BlockSpec((None, d, tc), lambda i, j, te, nu, live: (te[i], 0, chunk(i, j, nu))),
                pl.BlockSpec((None, 1, tc), lambda i, j, te, nu, live: (te[i], 0, chunk(i, j, nu))),
                pl.BlockSpec((None, tc, d), lambda i, j, te, nu, live: (te[i], chunk(i, j, nu), 0)),
                pl.BlockSpec((None, 1, d), lambda i, j, te, nu, live: (te[i], 0, 0)),
            ],
            out_specs=pl.BlockSpec((tm, d), lambda i, j, te, nu, live: (i, 0)),
            scratch_shapes=[pltpu.VMEM((tm, d), BF16)],
        ),
        compiler_params=_cparams(("arbitrary", "arbitrary"), BIG_VMEM_LIMIT),
        name="experts",
    )(tile_expert, n_used, tile_live, xs, wg, bg, wu, bu, wd, bd)


def _combine_body(dcur_ref, dnext_ref, gate_ref, h_ref, gfin_ref, ys_ref, o_ref, buf, sems, *, final_norm):
    i = pl.program_id(0)
    nb = pl.num_programs(0)
    tm = h_ref.shape[0]
    slot = i % 2

    def start_all(dref, s):
        def go(g, carry):
            t0 = pl.multiple_of(g * SUBLANES, SUBLANES)
            for r in range(SUBLANES):
                for k in range(TOP_K):
                    pltpu.make_async_copy(ys_ref.at[pl.ds(dref[0, (t0 + r) * TOP_K + k], 1), :],
                                          buf.at[s, k, pl.ds(t0 + r, 1), :], sems.at[s]).start()
            return carry
        lax.fori_loop(0, tm // SUBLANES, go, 0)

    @pl.when(i == 0)
    def _():
        start_all(dcur_ref, 0)

    @pl.when(i + 1 < nb)
    def _():
        start_all(dnext_ref, 1 - slot)

    for k in range(TOP_K):
        pltpu.make_async_copy(ys_ref.at[pl.ds(0, tm), :], buf.at[slot, k], sems.at[slot]).wait()

    y = h_ref[...]
    gates = gate_ref[...]
    for k in range(TOP_K):
        y = y + gates[:, k:k + 1] * buf[slot, k]
    if final_norm:
        ms = jnp.mean(y * y, axis=-1, keepdims=True)
        y = y * lax.rsqrt(ms + NORM_EPS) * gfin_ref[...]
    o_ref[...] = y


def _combine(dest, gates, h, g_final, ys, *, final_norm):
    n, d = h.shape
    tm = COMBINE_TM
    nb = n // tm
    dest3 = dest.reshape(nb, 1, tm * TOP_K)
    smem_blk = (None, 1, tm * TOP_K)
    return pl.pallas_call(
        functools.partial(_combine_body, final_norm=final_norm),
        out_shape=jax.ShapeDtypeStruct((n, d), F32),
        grid=(nb,),
        in_specs=[
            pl.BlockSpec(smem_blk, lambda i: (i, 0, 0), memory_space=pltpu.SMEM),
            pl.BlockSpec(smem_blk, lambda i: (jnp.minimum(i + 1, nb - 1), 0, 0), memory_space=pltpu.SMEM),
            pl.BlockSpec((tm, LANES), lambda i: (i, 0)),
            pl.BlockSpec((tm, d), lambda i: (i, 0)),
            pl.BlockSpec((1, d), lambda i: (0, 0)),
            pl.BlockSpec(memory_space=pl.ANY),
        ],
        out_specs=pl.BlockSpec((tm, d), lambda i: (i, 0)),
        scratch_shapes=[pltpu.VMEM((2, TOP_K, tm, d), F32), pltpu.SemaphoreType.DMA((2,))],
        compiler_params=_cparams(("arbitrary",)),
        name="combine",
    )(dest3, dest3, gates, h, g_final, ys)


def _rope_tables(positions):
    inv_freq = ROPE_THETA ** (-jnp.arange(0, HEAD_DIM, 2, dtype=F32) / HEAD_DIM)
    ang = positions.astype(F32)[..., None] * inv_freq
    ang = jnp.concatenate([ang, ang], axis=-1)
    sign = jnp.concatenate([-jnp.ones((HEAD_DIM // 2,), F32), jnp.ones((HEAD_DIM // 2,), F32)])
    return jnp.cos(ang), jnp.sin(ang) * sign


def _layer(x2, cos, sin_signed, layer_idx, bsz, seq, g_attn_norm, w_in, b_forget, lambda_q1, lambda_k1,
           lambda_q2, lambda_k2, g_diff_sub, g_fox_out, w_out, g_ffn_norm, w_router, b_router,
           w_gate, b_gate, w_up, b_up, w_down, b_down):
    n, d = x2.shape
    diff_w = d // 2
    fox_w = d - diff_w
    fox_heads = fox_w // HEAD_DIM
    main_cols = 3 * diff_w + 3 * fox_w
    n_experts = w_router.shape[1]
    lam_init = _lambda_init(layer_idx)

    d_exp = w_gate.shape[2]
    hosted = {"diff": (w_down.reshape(n_experts * d_exp, d), w_up.reshape(n_experts * d, d_exp)),
              "fox": (w_gate.reshape(n_experts * d, d_exp),)}
    host_steps = {"diff": bsz * (diff_w // (2 * HEAD_DIM)) * (seq // min(DIFF_T, seq)),
                  "fox": bsz * fox_heads * (seq // min(FOX_T, seq))}
    fused = {k: _can_cast_rows(ws, host_steps[k]) for k, ws in hosted.items()}

    w_in_t = jnp.swapaxes(w_in, 0, 1)
    w_main_t = w_in_t[:main_cols].astype(BF16)
    w_f_hi, w_f_lo = _split_bf16(jnp.pad(w_in_t[main_cols:], ((0, LANES - fox_heads), (0, 0))))
    proj, f_logit = _inproj(x2, g_attn_norm[None, :], w_main_t, w_f_hi, w_f_lo, cos, sin_signed,
                            diff_w=diff_w, fox_w=fox_w)

    z_rows = f_logit[:, :fox_heads].reshape(bsz, seq, fox_heads).transpose(0, 2, 1).reshape(bsz * fox_heads, seq)
    b_rows = jnp.tile(b_forget.astype(F32), bsz)[:, None]
    c_rows = _cumgate(z_rows, b_rows)

    od, *diff_cast = _diff_attention(proj, lambda_q1[None, :], lambda_k1[None, :], lambda_q2[None, :],
                                     lambda_k2[None, :], g_diff_sub[None, :], hosted["diff"] if fused["diff"] else (),
                                     bsz=bsz, seq=seq, diff_w=diff_w, lam_init=lam_init)
    c_keys = (c_rows * LOG2E).reshape(bsz * fox_heads, seq // LANES, LANES)
    of, *fox_cast = _fox_attention(proj, c_keys, g_fox_out[None, :], hosted["fox"] if fused["fox"] else (),
                                   bsz=bsz, seq=seq, diff_w=diff_w, fox_w=fox_w)
    w_down_b, w_up_b = diff_cast if fused["diff"] else (w.astype(BF16) for w in hosted["diff"])
    (w_gate_b,) = fox_cast if fused["fox"] else (w.astype(BF16) for w in hosted["fox"])
    w_down_b, w_up_b, w_gate_b = (w_down_b.reshape(w_down.shape), w_up_b.reshape(w_up.shape),
                                  w_gate_b.reshape(w_gate.shape))

    w_out_b = w_out.astype(BF16)
    w_router_hi, w_router_lo = _split_bf16(jnp.pad(w_router, ((0, 0), (0, LANES - n_experts))))
    b_router_p = jnp.pad(b_router, (0, LANES - n_experts))[None, :]
    h, hn, logits = _outproj(od, of, w_out_b[:diff_w], w_out_b[diff_w:], x2, g_ffn_norm[None, :],
                             w_router_hi, w_router_lo, b_router_p)

    e_pad, gate_pad, rank_pad, cnt = _route(logits, n_experts)
    e_idx = e_pad[:, :TOP_K]
    counts = cnt[0, :n_experts].astype(jnp.int32)
    padded = (counts + EXPERT_TM - 1) // EXPERT_TM * EXPERT_TM
    pad_ends = jnp.cumsum(padded)
    pad_starts = pad_ends - padded
    start_of = jnp.sum(jnp.where(e_idx[..., None] == jnp.arange(n_experts), pad_starts, 0), axis=-1)
    dest = (start_of + rank_pad[:, :TOP_K]).astype(jnp.int32)
    n_tiles = -(-n * TOP_K // EXPERT_TM) + n_experts
    tile_start = jnp.arange(n_tiles, dtype=jnp.int32) * EXPERT_TM
    tile_expert = jnp.sum((pad_ends[None, :] <= tile_start[:, None]).astype(jnp.int32), axis=1)
    tile_expert = jnp.minimum(tile_expert, n_experts - 1)
    n_used = (pad_ends[-1:] // EXPERT_TM).astype(jnp.int32)
    tile_expert = jnp.where(jnp.arange(n_tiles) < n_used[0], tile_expert, tile_expert[n_used[0] - 1])
    last_tile = jnp.where(padded > 0, pad_ends // EXPERT_TM - 1, -1)
    tail_tile = n_used[0] + jnp.arange(n_tiles - n * TOP_K // EXPERT_TM, dtype=jnp.int32)
    tail_tile = jnp.where(tail_tile < n_tiles, tail_tile, -1)
    zero_tiles = jnp.concatenate([last_tile, tail_tile]).astype(jnp.int32)

    routed_end = (pad_starts + counts)[tile_expert]
    tile_live = jnp.clip(routed_end - tile_start, 0, EXPERT_TM)
    tile_live = jnp.where(jnp.arange(n_tiles) < n_used[0], tile_live, 0)
    tile_live = ((tile_live + EXPERT_RT - 1) // EXPERT_RT * EXPERT_RT).astype(jnp.int32)

    xs = _dispatch(zero_tiles, dest, hn, n_tiles * EXPERT_TM)
    ys = _experts(tile_expert, n_used, tile_live, xs, w_gate_b, b_gate[:, None, :], w_up_b, b_up[:, None, :],
                  w_down_b, b_down[:, None, :])
    return dest, gate_pad, h, ys


def kernel(x, positions, g_attn_norm, w_in, b_forget, lambda_q1, lambda_k1, lambda_q2, lambda_k2, g_diff_sub,
           g_fox_out, w_out, g_ffn_norm, w_router, b_router, w_gate, b_gate, w_up, b_up, w_down, b_down,
           g_final):
    bsz, seq, d = x.shape
    depth = w_in.shape[0]
    cos, sin_signed = _rope_tables(positions)
    cos = cos.reshape(bsz * seq, HEAD_DIM)
    sin_signed = sin_signed.reshape(bsz * seq, HEAD_DIM)
    x2 = x.reshape(bsz * seq, d)
    for l in range(depth):
        dest, gates, h, ys = _layer(
            x2, cos, sin_signed, l, bsz, seq, g_attn_norm[l], w_in[l], b_forget[l], lambda_q1[l], lambda_k1[l],
            lambda_q2[l], lambda_k2[l], g_diff_sub[l], g_fox_out[l], w_out[l], g_ffn_norm[l], w_router[l],
            b_router[l], w_gate[l], b_gate[l], w_up[l], b_up[l], w_down[l], b_down[l])
        x2 = _combine(dest, gates, h, g_final[None, :], ys, final_norm=(l == depth - 1))
    return x2.reshape(bsz, seq, d)
```

```python
import functools
import math

import jax
import jax.numpy as jnp
from jax import lax
from jax.experimental import pallas as pl
from jax.experimental.pallas import tpu as pltpu

F32 = jnp.float32
BF16 = jnp.bfloat16

HEAD_DIM = 128
CHUNK = 64
ROPE_THETA = 10000.0
NORM_EPS = 1e-5
NEG_BIG = -1e30
TOP_K = 4
SWIGLU_LIMIT = 7.0
SWIGLU_ALPHA = 1.702
LANES = 128
SUBLANES = 8
QK_SCALE = HEAD_DIM ** -0.5
LOG2E = math.log2(math.e)
ONES_ROWS = 16

INPROJ_TM = 1024
INPROJ_TN = 512
DIFF_T = 512
FOX_T = 1024
CAST_ROWS = 512
VT_CHUNK = 512
OUTPROJ_TM = 512
ROUTE_TM = 512
DISPATCH_TM = 256
EXPERT_TM = 512
EXPERT_RT = 128
EXPERT_TC = 1024
COMBINE_TM = 128
VMEM_LIMIT = 48 * 1024 * 1024
BIG_VMEM_LIMIT = 58 * 1024 * 1024


def _lambda_init(layer_idx):
    return 0.8 - 0.6 * math.exp(-0.3 * layer_idx)


def _cparams(sem, vmem_limit=VMEM_LIMIT):
    return pltpu.CompilerParams(dimension_semantics=sem, vmem_limit_bytes=vmem_limit)


def _split_bf16(x):
    hi = x.astype(BF16)
    return hi, (x - hi.astype(F32)).astype(BF16)


def _dot_nt(a, b):
    return lax.dot_general(a, b, (((1,), (1,)), ((), ())), preferred_element_type=F32)


def _dot3(x_hi, x_lo, w_hi, w_lo, dot=functools.partial(jnp.dot, preferred_element_type=F32)):
    return dot(x_hi, w_hi) + (dot(x_hi, w_lo) + dot(x_lo, w_hi))


def _cast_rows_specs(weights, step_of):
    in_specs, out_specs, out_shapes = [], [], []
    for w in weights:
        rows, cols = w.shape
        last = rows // CAST_ROWS - 1
        spec = pl.BlockSpec((CAST_ROWS, cols), lambda *idx, last=last: (jnp.minimum(step_of(*idx), last), 0))
        in_specs.append(spec)
        out_specs.append(spec)
        out_shapes.append(jax.ShapeDtypeStruct((rows, cols), BF16))
    return in_specs, out_specs, out_shapes


def _can_cast_rows(weights, steps):
    return all(w.shape[0] % CAST_ROWS == 0 and w.shape[0] // CAST_ROWS <= steps for w in weights)


def _linear_step(n_axes):
    step = pl.program_id(0)
    for axis in range(1, n_axes):
        step = step * pl.num_programs(axis) + pl.program_id(axis)
    return step


def _cast_rows(step, in_refs, out_refs, total_rows):
    for src, dst, rows in zip(in_refs, out_refs, total_rows):
        @pl.when(step < rows // CAST_ROWS)
        def _():
            dst[...] = src[...].astype(BF16)


def _inproj_body(x_ref, g_ref, w_ref, wfh_ref, wfl_ref, cos_ref, sin_ref, o_ref, fl_ref, xn_ref,
                 *, n_rope, tn, q_ranges):
    j = pl.program_id(1)

    @pl.when(j == 0)
    def _():
        x = x_ref[...]
        ms = jnp.mean(x * x, axis=-1, keepdims=True)
        xn = x * lax.rsqrt(ms + NORM_EPS) * g_ref[...]
        xn_hi, xn_lo = _split_bf16(xn)
        xn_ref[...] = xn_hi
        fl_ref[...] = _dot3(xn_hi, xn_lo, wfh_ref[...], wfl_ref[...], dot=_dot_nt)

    acc = _dot_nt(xn_ref[...], w_ref[...])
    col0 = j * tn
    is_q = (col0 >= q_ranges[0][0]) & (col0 < q_ranges[0][1])
    is_q = is_q | ((col0 >= q_ranges[1][0]) & (col0 < q_ranges[1][1]))
    scale = jnp.where(is_q, QK_SCALE * LOG2E, 1.0).astype(F32)

    @pl.when(j < n_rope)
    def _():
        cos = cos_ref[...] * scale
        sin = sin_ref[...] * scale
        for gi in range(tn // LANES):
            xg = acc[:, gi * LANES:(gi + 1) * LANES]
            rot = pltpu.roll(xg, HEAD_DIM // 2, 1)
            o_ref[:, gi * LANES:(gi + 1) * LANES] = (xg * cos + rot * sin).astype(BF16)

    @pl.when(j >= n_rope)
    def _():
        o_ref[...] = (acc * scale).astype(BF16)


def _inproj(x2, g, w_main_t, w_f_hi, w_f_lo, cos, sin_signed, *, diff_w, fox_w):
    n, d = x2.shape
    cols = w_main_t.shape[0]
    tm, tn = min(INPROJ_TM, n), INPROJ_TN
    n_rope = 2 * diff_w // tn
    q_ranges = ((0, diff_w), (3 * diff_w, 3 * diff_w + fox_w))
    body = functools.partial(_inproj_body, n_rope=n_rope, tn=tn, q_ranges=q_ranges)
    return pl.pallas_call(
        body,
        out_shape=(jax.ShapeDtypeStruct((n, cols), BF16), jax.ShapeDtypeStruct((n, LANES), F32)),
        grid=(n // tm, cols // tn),
        in_specs=[
            pl.BlockSpec((tm, d), lambda i, j: (i, 0)),
            pl.BlockSpec((1, d), lambda i, j: (0, 0)),
            pl.BlockSpec((tn, d), lambda i, j: (j, 0)),
            pl.BlockSpec((LANES, d), lambda i, j: (0, 0)),
            pl.BlockSpec((LANES, d), lambda i, j: (0, 0)),
            pl.BlockSpec((tm, LANES), lambda i, j: (i, 0)),
            pl.BlockSpec((tm, LANES), lambda i, j: (i, 0)),
        ],
        out_specs=(pl.BlockSpec((tm, tn), lambda i, j: (i, j)),
                   pl.BlockSpec((tm, LANES), lambda i, j: (i, 0))),
        scratch_shapes=[pltpu.VMEM((tm, d), BF16)],
        compiler_params=_cparams(("parallel", "arbitrary")),
        name="inproj",
    )(x2, g, w_main_t, w_f_hi, w_f_lo, cos, sin_signed)


def _cumgate_body(z_ref, b_ref, c_ref):
    z = z_ref[...] + b_ref[...]
    v = jnp.minimum(z, 0.0) - jnp.log(1.0 + jnp.exp(-jnp.abs(z)))
    seq = v.shape[1]
    lane = lax.broadcasted_iota(jnp.int32, v.shape, 1)
    shift = 1
    while shift < seq:
        v = v + jnp.where(lane >= shift, pltpu.roll(v, shift, 1), 0.0)
        shift *= 2
    c_ref[...] = v


def _cumgate(z_rows, b_rows):
    return pl.pallas_call(
        _cumgate_body,
        out_shape=jax.ShapeDtypeStruct(z_rows.shape, F32),
        name="cumgate",
    )(z_rows, b_rows)


def _build_vt(v_ref, vt_ref):
    seq, dv = v_ref.shape
    for c in range(seq // VT_CHUNK):
        blk = v_ref[c * VT_CHUNK:(c + 1) * VT_CHUNK, :].astype(F32)
        vt_ref[0:dv, c * VT_CHUNK:(c + 1) * VT_CHUNK] = blk.T.astype(BF16)
    vt_ref[dv:dv + ONES_ROWS, :] = jnp.ones((ONES_ROWS, seq), BF16)


def _softmax_step(s_t, vt_blk, m_ref, acc_ref):
    m_prev = m_ref[...]
    m_new = jnp.maximum(m_prev, jnp.max(s_t, axis=0, keepdims=True))
    alpha = jnp.exp2(m_prev - m_new)
    p_t = jnp.exp2(s_t - m_new).astype(BF16)
    acc_ref[...] = alpha * acc_ref[...] + jnp.dot(vt_blk, p_t, preferred_element_type=F32)
    m_ref[...] = m_new


def _sweep_keys(qi, t, step):
    def pair(i, carry):
        step(pl.multiple_of(i * (2 * t), 2 * t), 2 * t, False)
        return carry

    lax.fori_loop(0, qi // 2, pair, 0)

    @pl.when(qi % 2 == 1)
    def _():
        step(pl.multiple_of((qi - 1) * t, t), t, False)

    step(pl.multiple_of(qi * t, t), t, True)


def _normalised(acc, dv):
    return acc[0:dv] * (1.0 / acc[dv:dv + 1])


def _diff_body(lq1_ref, lk1_ref, lq2_ref, lk2_ref, gsub_ref, q1_ref, q2_ref, k1_ref, k2_ref, v_ref,
               *rest, t, lam_init, cast_rows):
    n_cast = len(cast_rows)
    cast_in, (o_ref, *cast_out), (vt, m1, a1, m2, a2) = rest[:n_cast], rest[n_cast:2 * n_cast + 1], rest[2 * n_cast + 1:]
    _cast_rows(_linear_step(3), cast_in, cast_out, cast_rows)
    qi = pl.program_id(2)
    dv = v_ref.shape[1]

    @pl.when(qi == 0)
    def _():
        _build_vt(v_ref, vt)

    for m_ref, a_ref in ((m1, a1), (m2, a2)):
        m_ref[...] = jnp.full(m_ref.shape, NEG_BIG, F32)
        a_ref[...] = jnp.zeros(a_ref.shape, F32)
    q1 = q1_ref[...]
    q2 = q2_ref[...]
    key_chunk = lax.broadcasted_iota(jnp.int32, (t, t), 0) // CHUNK
    qry_chunk = lax.broadcasted_iota(jnp.int32, (t, t), 1) // CHUNK
    visible = key_chunk <= qry_chunk

    def step(off, size, diagonal):
        vt_blk = vt[:, pl.ds(off, size)]
        for q, k_ref, m_ref, a_ref in ((q1, k1_ref, m1, a1), (q2, k2_ref, m2, a2)):
            s_t = _dot_nt(k_ref[pl.ds(off, size), :], q)
            if diagonal:
                s_t = jnp.where(visible, s_t, NEG_BIG)
            _softmax_step(s_t, vt_blk, m_ref, a_ref)

    _sweep_keys(qi, t, step)

    lam = (jnp.exp(jnp.sum(lq1_ref[...] * lk1_ref[...], axis=-1, keepdims=True))
           - jnp.exp(jnp.sum(lq2_ref[...] * lk2_ref[...], axis=-1, keepdims=True)) + lam_init)
    o = (_normalised(a1[...], dv) - lam * _normalised(a2[...], dv)).T
    ms = jnp.mean(o * o, axis=-1, keepdims=True)
    y = o * lax.rsqrt(ms + NORM_EPS) * gsub_ref[...]
    o_ref[...] = (y * (1.0 - lam_init)).astype(BF16)


def _diff_attention(proj, lq1, lk1, lq2, lk2, gsub, cast_weights, *, bsz, seq, diff_w, lam_init):
    t = min(DIFF_T, seq)
    dv = 2 * HEAD_DIM
    n_heads = diff_w // dv
    nq = seq // t
    kcol = diff_w // HEAD_DIM
    vcol = 2 * diff_w // dv
    vec = pl.BlockSpec((1, HEAD_DIM), lambda b, h, i: (0, 0))
    cast_in, cast_out, cast_shapes = _cast_rows_specs(cast_weights, lambda b, h, i: (b * n_heads + h) * nq + i)
    body = functools.partial(_diff_body, t=t, lam_init=lam_init, cast_rows=tuple(w.shape[0] for w in cast_weights))
    return pl.pallas_call(
        body,
        out_shape=(jax.ShapeDtypeStruct((bsz * seq, diff_w), BF16), *cast_shapes),
        grid=(bsz, n_heads, nq),
        in_specs=[
            vec, vec, vec, vec,
            pl.BlockSpec((1, dv), lambda b, h, i: (0, 0)),
            pl.BlockSpec((t, HEAD_DIM), lambda b, h, i: (b * nq + i, 2 * h)),
            pl.BlockSpec((t, HEAD_DIM), lambda b, h, i: (b * nq + i, 2 * h + 1)),
            pl.BlockSpec((seq, HEAD_DIM), lambda b, h, i: (b, kcol + 2 * h)),
            pl.BlockSpec((seq, HEAD_DIM), lambda b, h, i: (b, kcol + 2 * h + 1)),
            pl.BlockSpec((seq, dv), lambda b, h, i: (b, vcol + h)),
            *cast_in,
        ],
        out_specs=(pl.BlockSpec((t, dv), lambda b, h, i: (b * nq + i, h)), *cast_out),
        scratch_shapes=[pltpu.VMEM((dv + ONES_ROWS, seq), BF16),
                        pltpu.VMEM((1, t), F32), pltpu.VMEM((dv + ONES_ROWS, t), F32),
                        pltpu.VMEM((1, t), F32), pltpu.VMEM((dv + ONES_ROWS, t), F32)],
        compiler_params=_cparams(("arbitrary", "arbitrary", "arbitrary"), BIG_VMEM_LIMIT),
        name="diffattn",
    )(lq1, lk1, lq2, lk2, gsub, proj, proj, proj, proj, proj, *cast_weights)


def _fox_body(g_ref, q_ref, k_ref, ck_ref, v_ref, *rest, t, cast_rows):
    n_cast = len(cast_rows)
    cast_in, (o_ref, *cast_out), (vt, ck_lanes, m, acc) = rest[:n_cast], rest[n_cast:2 * n_cast + 1], rest[2 * n_cast + 1:]
    _cast_rows(_linear_step(3), cast_in, cast_out, cast_rows)
    qi = pl.program_id(2)
    dv = v_ref.shape[1]

    @pl.when(qi == 0)
    def _():
        _build_vt(v_ref, vt)
        for j in range(ck_ref.shape[0]):
            ck_lanes[j * LANES:(j + 1) * LANES, :] = jnp.broadcast_to(ck_ref[j:j + 1, :], (LANES, LANES)).T

    m[...] = jnp.full(m.shape, NEG_BIG, F32)
    acc[...] = jnp.zeros(acc.shape, F32)
    q = q_ref[...]
    causal = lax.broadcasted_iota(jnp.int32, (t, t), 0) <= lax.broadcasted_iota(jnp.int32, (t, t), 1)

    def step(off, size, diagonal):
        ck = ck_lanes[pl.ds(off, size), :]
        s_t = _dot_nt(k_ref[pl.ds(off, size), :], q) - jnp.concatenate([ck] * (t // LANES), axis=1)
        if diagonal:
            s_t = jnp.where(causal, s_t, NEG_BIG)
        _softmax_step(s_t, vt[:, pl.ds(off, size)], m, acc)

    _sweep_keys(qi, t, step)

    o = _normalised(acc[...], dv).T
    ms = jnp.mean(o * o, axis=-1, keepdims=True)
    o_ref[...] = (o * lax.rsqrt(ms + NORM_EPS) * g_ref[...]).astype(BF16)


def _fox_attention(proj, c_keys, g_fox, cast_weights, *, bsz, seq, diff_w, fox_w):
    t = min(FOX_T, seq)
    n_heads = fox_w // HEAD_DIM
    nq = seq // t
    qcol = 3 * diff_w // HEAD_DIM
    kcol = qcol + n_heads
    vcol = kcol + n_heads
    cast_in, cast_out, cast_shapes = _cast_rows_specs(cast_weights, lambda b, h, i: (b * n_heads + h) * nq + i)
    body = functools.partial(_fox_body, t=t, cast_rows=tuple(w.shape[0] for w in cast_weights))
    return pl.pallas_call(
        body,
        out_shape=(jax.ShapeDtypeStruct((bsz * seq, fox_w), BF16), *cast_shapes),
        grid=(bsz, n_heads, nq),
        in_specs=[
            pl.BlockSpec((1, HEAD_DIM), lambda b, h, i: (0, 0)),
            pl.BlockSpec((t, HEAD_DIM), lambda b, h, i: (b * nq + i, qcol + h)),
            pl.BlockSpec((seq, HEAD_DIM), lambda b, h, i: (b, kcol + h)),
            pl.BlockSpec((None, seq // LANES, LANES), lambda b, h, i: (b * n_heads + h, 0, 0)),
            pl.BlockSpec((seq, HEAD_DIM), lambda b, h, i: (b, vcol + h)),
            *cast_in,
        ],
        out_specs=(pl.BlockSpec((t, HEAD_DIM), lambda b, h, i: (b * nq + i, h)), *cast_out),
        scratch_shapes=[pltpu.VMEM((HEAD_DIM + ONES_ROWS, seq), BF16), pltpu.VMEM((seq, LANES), F32),
                        pltpu.VMEM((1, t), F32), pltpu.VMEM((HEAD_DIM + ONES_ROWS, t), F32)],
        compiler_params=_cparams(("arbitrary", "arbitrary", "arbitrary"), BIG_VMEM_LIMIT),
        name="foxattn",
    )(g_fox, proj, proj, c_keys, proj, *cast_weights)


def _outproj_body(od_ref, of_ref, wd_ref, wf_ref, x_ref, g_ref, wrh_ref, wrl_ref, br_ref, h_ref, hn_ref, lg_ref):
    h = (x_ref[...]
         + jnp.dot(od_ref[...], wd_ref[...], preferred_element_type=F32)
         + jnp.dot(of_ref[...], wf_ref[...], preferred_element_type=F32))
    h_ref[...] = h
    ms = jnp.mean(h * h, axis=-1, keepdims=True)
    hn = h * lax.rsqrt(ms + NORM_EPS) * g_ref[...]
    hn_ref[...] = hn
    hn_hi, hn_lo = _split_bf16(hn)
    lg_ref[...] = _dot3(hn_hi, hn_lo, wrh_ref[...], wrl_ref[...]) + br_ref[...]


def _outproj(od, of, w_od, w_of, x2, g_ffn, w_router_hi, w_router_lo, b_router_p):
    n, d = x2.shape
    tm = OUTPROJ_TM
    row = lambda i: (i, 0)
    const = lambda i: (0, 0)
    resident = functools.partial(pl.BlockSpec, index_map=const, pipeline_mode=pl.Buffered(1))
    return pl.pallas_call(
        _outproj_body,
        out_shape=(jax.ShapeDtypeStruct((n, d), F32), jax.ShapeDtypeStruct((n, d), F32),
                   jax.ShapeDtypeStruct((n, LANES), F32)),
        grid=(n // tm,),
        in_specs=[
            pl.BlockSpec((tm, od.shape[1]), row),
            pl.BlockSpec((tm, of.shape[1]), row),
            resident(w_od.shape),
            resident(w_of.shape),
            pl.BlockSpec((tm, d), row),
            pl.BlockSpec((1, d), const),
            resident((d, LANES)),
            resident((d, LANES)),
            pl.BlockSpec((1, LANES), const),
        ],
        out_specs=(pl.BlockSpec((tm, d), row), pl.BlockSpec((tm, d), row), pl.BlockSpec((tm, LANES), row)),
        compiler_params=_cparams(("parallel",)),
        name="outproj",
    )(od, of, w_od, w_of, x2, g_ffn, w_router_hi, w_router_lo, b_router_p)


def _route_body(lg_ref, e_ref, gate_ref, rank_ref, cnt_ref, carry_ref, *, n_experts):
    i = pl.program_id(0)
    tm = lg_ref.shape[0]

    @pl.when(i == 0)
    def _():
        carry_ref[...] = jnp.zeros(carry_ref.shape, F32)

    lane = lax.broadcasted_iota(jnp.int32, (tm, LANES), 1)
    logits = jnp.where(lane < n_experts, lg_ref[...], -jnp.inf)
    vals, idxs, hots = [], [], []
    for _ in range(TOP_K):
        mx = jnp.max(logits, axis=-1, keepdims=True)
        idx = jnp.min(jnp.where(logits == mx, lane, LANES), axis=-1, keepdims=True)
        hot = lane == idx
        logits = jnp.where(hot, -jnp.inf, logits)
        vals.append(mx)
        idxs.append(idx)
        hots.append(hot)

    exps = [jnp.exp(v - vals[0]) for v in vals]
    denom = exps[0]
    for ex in exps[1:]:
        denom = denom + ex
    inv = 1.0 / denom

    hot_sum = jnp.zeros((tm, LANES), F32)
    for hot in hots:
        hot_sum = hot_sum + jnp.where(hot, 1.0, 0.0)
    r = lax.broadcasted_iota(jnp.int32, (tm, tm), 0)
    c = lax.broadcasted_iota(jnp.int32, (tm, tm), 1)
    earlier = jnp.where(c < r, 1.0, 0.0).astype(BF16)
    base = carry_ref[...] + jnp.dot(earlier, hot_sum.astype(BF16), preferred_element_type=F32)

    e_out = jnp.zeros((tm, LANES), jnp.int32)
    g_out = jnp.zeros((tm, LANES), F32)
    r_out = jnp.zeros((tm, LANES), jnp.int32)
    for k in range(TOP_K):
        rank = jnp.sum(jnp.where(hots[k], base, 0.0), axis=-1, keepdims=True).astype(jnp.int32)
        e_out = jnp.where(lane == k, idxs[k], e_out)
        g_out = jnp.where(lane == k, exps[k] * inv, g_out)
        r_out = jnp.where(lane == k, rank, r_out)
    e_ref[...] = e_out
    gate_ref[...] = g_out
    rank_ref[...] = r_out
    carry_ref[...] = carry_ref[...] + jnp.sum(hot_sum, axis=0, keepdims=True)
    cnt_ref[...] = carry_ref[...]


def _route(logits, n_experts):
    n = logits.shape[0]
    tm = ROUTE_TM
    row = lambda i: (i, 0)
    body = functools.partial(_route_body, n_experts=n_experts)
    return pl.pallas_call(
        body,
        out_shape=(jax.ShapeDtypeStruct((n, LANES), jnp.int32), jax.ShapeDtypeStruct((n, LANES), F32),
                   jax.ShapeDtypeStruct((n, LANES), jnp.int32), jax.ShapeDtypeStruct((1, LANES), F32)),
        grid=(n // tm,),
        in_specs=[pl.BlockSpec((tm, LANES), row)],
        out_specs=(pl.BlockSpec((tm, LANES), row), pl.BlockSpec((tm, LANES), row),
                   pl.BlockSpec((tm, LANES), row), pl.BlockSpec((1, LANES), lambda i: (0, 0))),
        scratch_shapes=[pltpu.VMEM((1, LANES), F32)],
        compiler_params=_cparams(("arbitrary",)),
        name="route",
    )(logits)


def _dispatch_body(ztile_ref, dest_ref, hn_ref, xs_ref, zeros, sem, zsem):
    tm = hn_ref.shape[0]

    @pl.when(pl.program_id(0) == 0)
    def _():
        zeros[...] = jnp.zeros(zeros.shape, zeros.dtype)

        def fill(z):
            row0 = pl.multiple_of(ztile_ref[z] * EXPERT_TM, EXPERT_TM)
            return pltpu.make_async_copy(zeros, xs_ref.at[pl.ds(row0, EXPERT_TM), :], zsem)

        def fill_start(z, carry):
            @pl.when(ztile_ref[z] >= 0)
            def _():
                fill(z).start()
            return carry

        def fill_wait(z, carry):
            @pl.when(ztile_ref[z] >= 0)
            def _():
                fill(z).wait()
            return carry

        lax.fori_loop(0, ztile_ref.shape[0], fill_start, 0)
        lax.fori_loop(0, ztile_ref.shape[0], fill_wait, 0)

    def start(t, carry):
        src = hn_ref.at[pl.ds(t, 1), :]
        for k in range(TOP_K):
            pltpu.make_async_copy(src, xs_ref.at[pl.ds(dest_ref[0, t * TOP_K + k], 1), :], sem).start(priority=k % 2)
        return carry

    lax.fori_loop(0, tm, start, 0)
    for _ in range(TOP_K):
        pltpu.make_async_copy(hn_ref, xs_ref.at[pl.ds(0, tm), :], sem).wait()


def _dispatch(zero_tiles, dest, hn, n_rows):
    n, d = hn.shape
    tm = DISPATCH_TM
    dest3 = dest.reshape(n // tm, 1, tm * TOP_K)
    return pl.pallas_call(
        _dispatch_body,
        out_shape=jax.ShapeDtypeStruct((n_rows, d), hn.dtype),
        grid_spec=pltpu.PrefetchScalarGridSpec(
            num_scalar_prefetch=1,
            grid=(n // tm,),
            in_specs=[
                pl.BlockSpec((None, 1, tm * TOP_K), lambda i, zt: (i, 0, 0), memory_space=pltpu.SMEM),
                pl.BlockSpec((tm, d), lambda i, zt: (i, 0)),
            ],
            out_specs=pl.BlockSpec(memory_space=pl.ANY),
            scratch_shapes=[pltpu.VMEM((EXPERT_TM, d), hn.dtype), pltpu.SemaphoreType.DMA,
                            pltpu.SemaphoreType.DMA],
        ),
        compiler_params=_cparams(("arbitrary",)),
        name="dispatch",
    )(zero_tiles, dest3, hn)


def _experts_body(te_ref, nu_ref, live_ref, x_ref, wg_ref, bg_ref, wu_ref, bu_ref, wd_ref, bd_ref, o_ref, xb_ref):
    i = pl.program_id(0)
    j = pl.program_id(1)
    tm = x_ref.shape[0]

    def compute(r):
        @pl.when(j == 0)
        def _():
            xb_ref[0:r, :] = x_ref[0:r, :].astype(BF16)
            o_ref[0:r, :] = jnp.broadcast_to(bd_ref[...], (r, o_ref.shape[1]))
            if r < tm:
                o_ref[r:tm, :] = jnp.zeros((tm - r, o_ref.shape[1]), F32)

        xb = xb_ref[0:r, :]
        g = jnp.dot(xb, wg_ref[...], preferred_element_type=F32) + bg_ref[...]
        u = jnp.dot(xb, wu_ref[...], preferred_element_type=F32) + bu_ref[...]
        g = jnp.minimum(g, SWIGLU_LIMIT)
        u = jnp.clip(u, -SWIGLU_LIMIT, SWIGLU_LIMIT)
        hdn = g * (1.0 / (1.0 + jnp.exp(-SWIGLU_ALPHA * g))) * (u + 1.0)
        o_ref[0:r, :] += jnp.dot(hdn.astype(BF16), wd_ref[...], preferred_element_type=F32)

    for r in range(EXPERT_RT, tm + 1, EXPERT_RT):
        pl.when(live_ref[i] == r)(functools.partial(compute, r))

    @pl.when((live_ref[i] == 0) & (j == 0))
    def _():
        o_ref[...] = jnp.zeros(o_ref.shape, F32)


def _experts(tile_expert, n_used, tile_live, xs, wg, bg, wu, bu, wd, bd):
    rows, d = xs.shape
    d_exp = wg.shape[2]
    tm, tc = EXPERT_TM, min(EXPERT_TC, d_exp)
    n_tiles = rows // tm
    nj = d_exp // tc

    def tile(i, nu):
        return jnp.minimum(i, nu[0] - 1)

    def chunk(i, j, nu):
        return jnp.where(i < nu[0], j, nj - 1)

    return pl.pallas_call(
        _experts_body,
        out_shape=jax.ShapeDtypeStruct((rows, d), F32),
        grid_spec=pltpu.PrefetchScalarGridSpec(
            num_scalar_prefetch=3,
            grid=(n_tiles, nj),
            in_specs=[
                pl.BlockSpec((tm, d), lambda i, j, te, nu, live: (tile(i, nu), 0)),
                pl.BlockSpec((None, d, tc), lambda i, j, te, nu, live: (te[i], 0, chunk(i, j, nu))),
                pl.BlockSpec((None, 1, tc), lambda i, j, te, nu, live: (te[i], 0, chunk(i, j, nu))),
                pl.BlockSpec((None, d, tc), lambda i, j, te, nu, live: (te[i], 0, chunk(i, j, nu))),
                pl.BlockSpec((None, 1, tc), lambda i, j, te, nu, live: (te[i], 0, chunk(i, j, nu))),
                pl.BlockSpec((None, tc, d), lambda i, j, te, nu, live: (te[i], chunk(i, j, nu), 0)),
                pl.BlockSpec((None, 1, d), lambda i, j, te, nu, live: (te[i], 0, 0)),
            ],
            out_specs=pl.BlockSpec((tm, d), lambda i, j, te, nu, live: (i, 0)),
            scratch_shapes=[pltpu.VMEM((tm, d), BF16)],
        ),
        compiler_params=_cparams(("arbitrary", "arbitrary"), BIG_VMEM_LIMIT),
        name="experts",
    )(tile_expert, n_used, tile_live, xs, wg, bg, wu, bu, wd, bd)


def _combine_body(dcur_ref, dnext_ref, gate_ref, h_ref, gfin_ref, ys_ref, o_ref, buf, sems, *, final_norm):
    i = pl.program_id(0)
    nb = pl.num_programs(0)
    tm = h_ref.shape[0]
    slot = i % 2

    def start_all(dref, s):
        def go(g, carry):
            t0 = pl.multiple_of(g * SUBLANES, SUBLANES)
            for r in range(SUBLANES):
                for k in range(TOP_K):
                    pltpu.make_async_copy(ys_ref.at[pl.ds(dref[0, (t0 + r) * TOP_K + k], 1), :],
                                          buf.at[s, k, pl.ds(t0 + r, 1), :], sems.at[s]).start(priority=k % 2)
            return carry
        lax.fori_loop(0, tm // SUBLANES, go, 0)

    @pl.when(i == 0)
    def _():
        start_all(dcur_ref, 0)

    @pl.when(i + 1 < nb)
    def _():
        start_all(dnext_ref, 1 - slot)

    for k in range(TOP_K):
        pltpu.make_async_copy(ys_ref.at[pl.ds(0, tm), :], buf.at[slot, k], sems.at[slot]).wait()

    y = h_ref[...]
    gates = gate_ref[...]
    for k in range(TOP_K):
        y = y + gates[:, k:k + 1] * buf[slot, k]
    if final_norm:
        ms = jnp.mean(y * y, axis=-1, keepdims=True)
        y = y * lax.rsqrt(ms + NORM_EPS) * gfin_ref[...]
    o_ref[...] = y


def _combine(dest, gates, h, g_final, ys, *, final_norm):
    n, d = h.shape
    tm = COMBINE_TM
    nb = n // tm
    dest3 = dest.reshape(nb, 1, tm * TOP_K)
    smem_blk = (None, 1, tm * TOP_K)
    return pl.pallas_call(
        functools.partial(_combine_body, final_norm=final_norm),
        out_shape=jax.ShapeDtypeStruct((n, d), F32),
        grid=(nb,),
        in_specs=[
            pl.BlockSpec(smem_blk, lambda i: (i, 0, 0), memory_space=pltpu.SMEM),
            pl.BlockSpec(smem_blk, lambda i: (jnp.minimum(i + 1, nb - 1), 0, 0), memory_space=pltpu.SMEM),
            pl.BlockSpec((tm, LANES), lambda i: (i, 0)),
            pl.BlockSpec((tm, d), lambda i: (i, 0)),
            pl.BlockSpec((1, d), lambda i: (0, 0)),
            pl.BlockSpec(memory_space=pl.ANY),
        ],
        out_specs=pl.BlockSpec((tm, d), lambda i: (i, 0)),
        scratch_shapes=[pltpu.VMEM((2, TOP_K, tm, d), F32), pltpu.SemaphoreType.DMA((2,))],
        compiler_params=_cparams(("arbitrary",)),
        name="combine",
    )(dest3, dest3, gates, h, g_final, ys)


def _rope_tables(positions):
    inv_freq = ROPE_THETA ** (-jnp.arange(0, HEAD_DIM, 2, dtype=F32) / HEAD_DIM)
    ang = positions.astype(F32)[..., None] * inv_freq
    ang = jnp.concatenate([ang, ang], axis=-1)
    sign = jnp.concatenate([-jnp.ones((HEAD_DIM // 2,), F32), jnp.ones((HEAD_DIM // 2,), F32)])
    return jnp.cos(ang), jnp.sin(ang) * sign


def _layer(x2, cos, sin_signed, layer_idx, bsz, seq, g_attn_norm, w_in, b_forget, lambda_q1, lambda_k1,
           lambda_q2, lambda_k2, g_diff_sub, g_fox_out, w_out, g_ffn_norm, w_router, b_router,
           w_gate, b_gate, w_up, b_up, w_down, b_down):
    n, d = x2.shape
    diff_w = d // 2
    fox_w = d - diff_w
    fox_heads = fox_w // HEAD_DIM
    main_cols = 3 * diff_w + 3 * fox_w
    n_experts = w_router.shape[1]
    lam_init = _lambda_init(layer_idx)

    d_exp = w_gate.shape[2]
    hosted = {"diff": (w_down.reshape(n_experts * d_exp, d), w_up.reshape(n_experts * d, d_exp)),
              "fox": (w_gate.reshape(n_experts * d, d_exp),)}
    host_steps = {"diff": bsz * (diff_w // (2 * HEAD_DIM)) * (seq // min(DIFF_T, seq)),
                  "fox": bsz * fox_heads * (seq // min(FOX_T, seq))}
    fused = {k: _can_cast_rows(ws, host_steps[k]) for k, ws in hosted.items()}

    w_in_t = jnp.swapaxes(w_in, 0, 1)
    w_main_t = w_in_t[:main_cols].astype(BF16)
    w_f_hi, w_f_lo = _split_bf16(jnp.pad(w_in_t[main_cols:], ((0, LANES - fox_heads), (0, 0))))
    proj, f_logit = _inproj(x2, g_attn_norm[None, :], w_main_t, w_f_hi, w_f_lo, cos, sin_signed,
                            diff_w=diff_w, fox_w=fox_w)

    z_rows = f_logit[:, :fox_heads].reshape(bsz, seq, fox_heads).transpose(0, 2, 1).reshape(bsz * fox_heads, seq)
    b_rows = jnp.tile(b_forget.astype(F32), bsz)[:, None]
    c_rows = _cumgate(z_rows, b_rows)

    od, *diff_cast = _diff_attention(proj, lambda_q1[None, :], lambda_k1[None, :], lambda_q2[None, :],
                                     lambda_k2[None, :], g_diff_sub[None, :], hosted["diff"] if fused["diff"] else (),
                                     bsz=bsz, seq=seq, diff_w=diff_w, lam_init=lam_init)
    c_keys = (c_rows * LOG2E).reshape(bsz * fox_heads, seq // LANES, LANES)
    of, *fox_cast = _fox_attention(proj, c_keys, g_fox_out[None, :], hosted["fox"] if fused["fox"] else (),
                                   bsz=bsz, seq=seq, diff_w=diff_w, fox_w=fox_w)
    w_down_b, w_up_b = diff_cast if fused["diff"] else (w.astype(BF16) for w in hosted["diff"])
    (w_gate_b,) = fox_cast if fused["fox"] else (w.astype(BF16) for w in hosted["fox"])
    w_down_b, w_up_b, w_gate_b = (w_down_b.reshape(w_down.shape), w_up_b.reshape(w_up.shape),
                                  w_gate_b.reshape(w_gate.shape))

    w_out_b = w_out.astype(BF16)
    w_router_hi, w_router_lo = _split_bf16(jnp.pad(w_router, ((0, 0), (0, LANES - n_experts))))
    b_router_p = jnp.pad(b_router, (0, LANES - n_experts))[None, :]
    h, hn, logits = _outproj(od, of, w_out_b[:diff_w], w_out_b[diff_w:], x2, g_ffn_norm[None, :],
                             w_router_hi, w_router_lo, b_router_p)

    e_pad, gate_pad, rank_pad, cnt = _route(logits, n_experts)
    e_idx = e_pad[:, :TOP_K]
    counts = cnt[0, :n_experts].astype(jnp.int32)
    padded = (counts + EXPERT_TM - 1) // EXPERT_TM * EXPERT_TM
    pad_ends = jnp.cumsum(padded)
    pad_starts = pad_ends - padded
    start_of = jnp.sum(jnp.where(e_idx[..., None] == jnp.arange(n_experts), pad_starts, 0), axis=-1)
    dest = (start_of + rank_pad[:, :TOP_K]).astype(jnp.int32)
    n_tiles = -(-n * TOP_K // EXPERT_TM) + n_experts
    tile_start = jnp.arange(n_tiles, dtype=jnp.int32) * EXPERT_TM
    tile_expert = jnp.sum((pad_ends[None, :] <= tile_start[:, None]).astype(jnp.int32), axis=1)
    tile_expert = jnp.minimum(tile_expert, n_experts - 1)
    n_used = (pad_ends[-1:] // EXPERT_TM).astype(jnp.int32)
    tile_expert = jnp.where(jnp.arange(n_tiles) < n_used[0], tile_expert, tile_expert[n_used[0] - 1])
    last_tile = jnp.where(padded > 0, pad_ends // EXPERT_TM - 1, -1)
    tail_tile = n_used[0] + jnp.arange(n_tiles - n * TOP_K // EXPERT_TM, dtype=jnp.int32)
    tail_tile = jnp.where(tail_tile < n_tiles, tail_tile, -1)
    zero_tiles = jnp.concatenate([last_tile, tail_tile]).astype(jnp.int32)

    routed_end = (pad_starts + counts)[tile_expert]
    tile_live = jnp.clip(routed_end - tile_start, 0, EXPERT_TM)
    tile_live = jnp.where(jnp.arange(n_tiles) < n_used[0], tile_live, 0)
    tile_live = ((tile_live + EXPERT_RT - 1) // EXPERT_RT * EXPERT_RT).astype(jnp.int32)

    xs = _dispatch(zero_tiles, dest, hn, n_tiles * EXPERT_TM)
    ys = _experts(tile_expert, n_used, tile_live, xs, w_gate_b, b_gate[:, None, :], w_up_b, b_up[:, None, :],
                  w_down_b, b_down[:, None, :])
    return dest, gate_pad, h, ys


def kernel(x, positions, g_attn_norm, w_in, b_forget, lambda_q1, lambda_k1, lambda_q2, lambda_k2, g_diff_sub,
           g_fox_out, w_out, g_ffn_norm, w_router, b_router, w_gate, b_gate, w_up, b_up, w_down, b_down,
           g_final):
    bsz, seq, d = x.shape
    depth = w_in.shape[0]
    cos, sin_signed = _rope_tables(positions)
    cos = cos.reshape(bsz * seq, HEAD_DIM)
    sin_signed = sin_signed.reshape(bsz * seq, HEAD_DIM)
    x2 = x.reshape(bsz * seq, d)
    for l in range(depth):
        dest, gates, h, ys = _layer(
            x2, cos, sin_signed, l, bsz, seq, g_attn_norm[l], w_in[l], b_forget[l], lambda_q1[l], lambda_k1[l],
            lambda_q2[l], lambda_k2[l], g_diff_sub[l], g_fox_out[l], w_out[l], g_ffn_norm[l], w_router[l],
            b_router[l], w_gate[l], b_gate[l], w_up[l], b_up[l], w_down[l], b_down[l])
        x2 = _combine(dest, gates, h, g_final[None, :], ys, final_norm=(l == depth - 1))
    return x2.reshape(bsz, seq, d)
```
